```python
import numpy as np
import jax
import jax.numpy as jnp
from jax import lax

D_MODEL = 1024
BATCH = 32
SEQ = 256
DEPTH = 2
DEC_BATCH = 8
DEC_SEQ = 4096
PAST_LEN = 256

GRID_W = 64
N_MOD = 9
FFN_DIM = 2816
BRANCH_W = 512
NA_HEADS = 8
NA_HEAD_DIM = 64
NA_WIN_H = 8
NA_WIN_W = 16
NA_QBLK = 16
NA_SLAB_W = NA_QBLK + NA_WIN_W
POOL_GROUPS = 4
POOL_DIM = 512
POOL_GROUP_DIM = POOL_DIM // POOL_GROUPS
POOL_WINDOWS = (2, 4, 8, 16)
MLA_HEADS = 8
MLA_Q_LORA = 256
MLA_KV_LORA = 256
MLA_NOPE = 64
MLA_ROPE = 32
MLA_V = 64
ROPE_BASE = 10000.0
ATTN_QBLOCK = 128
NORM_EPS = 1e-6
NEG_INF = -1e30
IN_SPLITS = (NA_HEADS * NA_HEAD_DIM, NA_HEADS * NA_HEAD_DIM, NA_HEADS * NA_HEAD_DIM, POOL_DIM, MLA_Q_LORA, MLA_KV_LORA, MLA_ROPE, 3 * D_MODEL)
IN_COLS = sum(IN_SPLITS)

kernel_name = 'hybrid_flow_na_pool_mla_step'


def rmsnorm(x, g):
    xf = x.astype(jnp.float32)
    y = xf * lax.rsqrt(jnp.mean(xf * xf, axis=-1, keepdims=True) + NORM_EPS)
    return (y * g.astype(jnp.float32)).astype(x.dtype)


def modulations(cvec, w_ada, b_ada):
    m = jax.nn.silu(cvec) @ w_ada + b_ada
    return m.reshape(cvec.shape[0], 1, N_MOD, D_MODEL)


def swiglu(u, w_in, w_out):
    gate, up = jnp.split(u @ w_in, 2, axis=-1)
    return (jax.nn.silu(gate) * up) @ w_out


def axial_rope(x, n_tok):
    half = MLA_ROPE // 2
    n_freq = half // 2
    inv = ROPE_BASE ** (-jnp.arange(n_freq, dtype=jnp.float32) / n_freq)
    t = jnp.arange(n_tok)
    bshape = (n_tok,) + (1,) * (x.ndim - 3) + (n_freq,)
    ang_r = ((t // GRID_W).astype(jnp.float32)[:, None] * inv).reshape(bshape)
    ang_c = ((t % GRID_W).astype(jnp.float32)[:, None] * inv).reshape(bshape)
    xf = x.astype(jnp.float32)

    def rot(xa, ang):
        x1, x2 = jnp.split(xa, 2, axis=-1)
        cos, sin = jnp.cos(ang), jnp.sin(ang)
        return jnp.concatenate([x1 * cos - x2 * sin, x2 * cos + x1 * sin], axis=-1)

    return jnp.concatenate([rot(xf[..., :half], ang_r), rot(xf[..., half:], ang_c)], axis=-1).astype(x.dtype)


def blocked_attention(q, k, v):
    b, nq, h, dk = q.shape
    nb = nq // ATTN_QBLOCK
    qb = q.reshape(b, nb, ATTN_QBLOCK, h, dk).transpose(1, 0, 2, 3, 4)

    def one_block(qi):
        s = jnp.einsum('bqhd,bkhd->bhqk', qi, k).astype(jnp.float32)
        p = jax.nn.softmax(s, axis=-1).astype(v.dtype)
        return jnp.einsum('bhqk,bkhd->bqhd', p, v)

    o = lax.map(one_block, qb)
    return o.transpose(1, 0, 2, 3, 4).reshape(b, nq, h, v.shape[-1])


def neighborhood_attention(q, k, v, k_ctx, v_ctx, rpb):
    b, n, h, dh = q.shape
    rows = n // GRID_W
    wr = min(NA_WIN_H, rows)
    ncb = GRID_W // NA_QBLK
    qg = q.reshape(b, rows, ncb, NA_QBLK, h, dh)
    kg = k.reshape(b, rows, GRID_W, h, dh)
    vg = v.reshape(b, rows, GRID_W, h, dh)
    row_start = jnp.clip(jnp.arange(rows) - wr // 2, 0, rows - wr)
    q_cols = jnp.arange(GRID_W).reshape(ncb, NA_QBLK)
    slab_start = jnp.clip(jnp.arange(ncb) * NA_QBLK - NA_WIN_W // 2, 0, GRID_W - NA_SLAB_W)
    slab_cols = slab_start[:, None] + jnp.arange(NA_SLAB_W)
    win_start = jnp.clip(q_cols - NA_WIN_W // 2, 0, GRID_W - NA_WIN_W)
    kc = slab_cols[:, None, :]
    in_win = (kc >= win_start[:, :, None]) & (kc < win_start[:, :, None] + NA_WIN_W)
    col_idx = jnp.clip(kc - q_cols[:, :, None] + NA_WIN_W - 1, 0, 2 * NA_WIN_W - 2)
    n_loc = wr * NA_SLAB_W

    def one_row(r):
        rs = row_start[r]
        k_blk = lax.dynamic_slice_in_dim(kg, rs, wr, axis=1)[:, :, slab_cols]
        v_blk = lax.dynamic_slice_in_dim(vg, rs, wr, axis=1)[:, :, slab_cols]
        q_blk = qg[:, r]
        s_loc = jnp.einsum('bnqhd,bwnshd->bhnqws', q_blk, k_blk).astype(jnp.float32)
        rel_r = rs + jnp.arange(wr) - r + NA_WIN_H - 1
        bias = rpb[:, rel_r[None, None, :, None], col_idx[:, :, None, :]]
        s_loc = jnp.where(in_win[:, :, None, :], s_loc + bias.astype(jnp.float32), NEG_INF)
        s_loc = s_loc.reshape(b, h, ncb, NA_QBLK, n_loc)
        s_ctx = jnp.einsum('bnqhd,bphd->bhnqp', q_blk, k_ctx).astype(jnp.float32)
        p = jax.nn.softmax(jnp.concatenate([s_loc, s_ctx], axis=-1), axis=-1).astype(v.dtype)
        v_loc = v_blk.transpose(0, 2, 1, 3, 4, 5).reshape(b, ncb, n_loc, h, dh)
        return (jnp.einsum('bhnqs,bnshd->bnqhd', p[..., :n_loc], v_loc)
                + jnp.einsum('bhnqp,bphd->bnqhd', p[..., n_loc:], v_ctx))

    o = lax.map(one_row, jnp.arange(rows))
    return o.transpose(1, 0, 2, 3, 4, 5).reshape(b, n, h, dh)


def pool_mix(p_in, pool_w, pool_scale):
    b, n, _ = p_in.shape
    xg = p_in.astype(jnp.float32).reshape(b, n, POOL_GROUPS, POOL_GROUP_DIM)
    csum = jnp.concatenate([jnp.zeros((b, 1, POOL_GROUPS, POOL_GROUP_DIM), jnp.float32), jnp.cumsum(xg, axis=1)], axis=1)
    t = jnp.arange(n)
    outs = []
    for g, w in enumerate(POOL_WINDOWS):
        lo = jnp.clip(t - w // 2, 0, n)
        hi = jnp.clip(t + w // 2, 0, n)
        cg = csum[:, :, g]
        mean = (cg[:, hi] - cg[:, lo]) / (hi - lo).astype(jnp.float32)[:, None]
        outs.append(mean - xg[:, :, g])
    y = jnp.stack(outs, axis=2).astype(p_in.dtype)
    y = jnp.einsum('bngc,gcd->bngd', y, pool_w)
    return y.reshape(b, n, POOL_DIM) * pool_scale


def split_cols(h):
    bounds = [int(i) for i in np.cumsum(IN_SPLITS)[:-1]]
    return jnp.split(h, bounds, axis=-1)


def mla_queries(cq, q_norm, w_uq):
    b, n, _ = cq.shape
    q = (rmsnorm(cq, q_norm) @ w_uq).reshape(b, n, MLA_HEADS, MLA_NOPE + MLA_ROPE)
    return q[..., :MLA_NOPE], q[..., MLA_NOPE:]


def mla_kv(ckv, w_ukv):
    b, n, _ = ckv.shape
    kv = (ckv @ w_ukv).reshape(b, n, MLA_HEADS, MLA_NOPE + MLA_V)
    return kv[..., :MLA_NOPE], kv[..., MLA_NOPE:]


def mla_keys(k_nope, krope):
    kr = jnp.broadcast_to(krope[:, :, None, :], k_nope.shape[:-1] + (MLA_ROPE,))
    return jnp.concatenate([k_nope, kr], axis=-1)


def merge_branches(na_o, pool_o, mla_o, gates, w_branch, w_out):
    b, n, _ = gates.shape
    br = jnp.stack([na_o.reshape(b, n, BRANCH_W), pool_o, mla_o.reshape(b, n, BRANCH_W)], axis=2)
    y = jnp.einsum('bnkc,kcd->bnkd', br, w_branch)
    g = jax.nn.sigmoid(gates.reshape(b, n, 3, D_MODEL))
    return jnp.sum(g * y, axis=2) @ w_out


def context_mixer(u, w_in, pool_w, pool_scale, q_norm, kv_norm, w_uq, w_ukv, w_branch, w_out):
    b, n, _ = u.shape
    na_q, na_k, na_v, pool_in, cq, ckv_raw, krope, gates = split_cols(u @ w_in)
    na_q = na_q.reshape(b, n, NA_HEADS, NA_HEAD_DIM) * (NA_HEAD_DIM ** -0.5)
    na_k = na_k.reshape(b, n, NA_HEADS, NA_HEAD_DIM)
    na_v = na_v.reshape(b, n, NA_HEADS, NA_HEAD_DIM)
    na_o = blocked_attention(na_q, na_k, na_v)
    pool_o = pool_mix(pool_in, pool_w, pool_scale)
    ckv = rmsnorm(ckv_raw, kv_norm)
    q_nope, q_rope = mla_queries(cq, q_norm, w_uq)
    k_nope, mv = mla_kv(ckv, w_ukv)
    q = jnp.concatenate([q_nope, q_rope], axis=-1) * ((MLA_NOPE + MLA_ROPE) ** -0.5)
    mla_o = blocked_attention(q, mla_keys(k_nope, krope), mv)
    out = merge_branches(na_o, pool_o, mla_o, gates, w_branch, w_out)
    return out, (na_k, na_v, ckv, krope)


def latent_mixer(u, na_kc, na_vc, ckv_c, krope_c, w_in, na_rpb, pool_w, pool_scale, q_norm, kv_norm, w_uq, w_ukv, w_branch, w_out):
    b, n, _ = u.shape
    na_q, na_k, na_v, pool_in, cq, ckv_raw, krope, gates = split_cols(u @ w_in)
    na_q = na_q.reshape(b, n, NA_HEADS, NA_HEAD_DIM) * (NA_HEAD_DIM ** -0.5)
    na_k = na_k.reshape(b, n, NA_HEADS, NA_HEAD_DIM)
    na_v = na_v.reshape(b, n, NA_HEADS, NA_HEAD_DIM)
    na_o = neighborhood_attention(na_q, na_k, na_v, na_kc, na_vc, na_rpb)
    pool_o = pool_mix(pool_in, pool_w, pool_scale)
    q_nope, q_rope = mla_queries(cq, q_norm, w_uq)
    q = jnp.concatenate([q_nope, axial_rope(q_rope, n)], axis=-1) * ((MLA_NOPE + MLA_ROPE) ** -0.5)
    ckv_all = jnp.concatenate([rmsnorm(ckv_raw, kv_norm), ckv_c], axis=1)
    krope_all = jnp.concatenate([axial_rope(krope, n), krope_c], axis=1)
    k_nope, mv = mla_kv(ckv_all, w_ukv)
    mla_o = blocked_attention(q, mla_keys(k_nope, krope_all), mv)
    return merge_branches(na_o, pool_o, mla_o, gates, w_branch, w_out)


def trunk_layer(x, mods, mixer_fn, norm_pre, norm_post, ffn_w_in, ffn_w_out):
    def pre(h, j):
        return rmsnorm(h, norm_pre[j]) * (1 + mods[:, :, 3 * j + 1]) + mods[:, :, 3 * j]

    def post(y, j):
        return mods[:, :, 3 * j + 2] * rmsnorm(y, norm_post[j])

    x = x + 0.5 * post(swiglu(pre(x, 0), ffn_w_in[0], ffn_w_out[0]), 0)
    y, aux = mixer_fn(pre(x, 1))
    x = x + post(y, 1)
    x = x + 0.5 * post(swiglu(pre(x, 2), ffn_w_in[1], ffn_w_out[1]), 2)
    return x, aux


def setup_inputs(seed: int = 0) -> dict:
    key = jax.random.key(seed)
    ks = jax.random.split(key, 24)

    def nrm(k, shape, scale=1.0):
        return scale * jax.random.normal(k, shape, jnp.float32)

    D = D_MODEL
    return {
        'x_prompt': nrm(ks[0], (BATCH, SEQ, D)),
        'x_sample': nrm(ks[1], (DEC_BATCH, DEC_SEQ, D)),
        'cache_na_k': nrm(ks[2], (DEC_BATCH, DEPTH, PAST_LEN, NA_HEADS, NA_HEAD_DIM)),
        'cache_na_v': nrm(ks[3], (DEC_BATCH, DEPTH, PAST_LEN, NA_HEADS, NA_HEAD_DIM)),
        'cache_mla_ckv': nrm(ks[4], (DEC_BATCH, DEPTH, PAST_LEN, MLA_KV_LORA)),
        'cache_mla_krope': nrm(ks[5], (DEC_BATCH, DEPTH, PAST_LEN, MLA_ROPE)),
        'c': nrm(ks[6], (DEC_BATCH, D)),
        'c_ctx': nrm(ks[7], (D,)),
        'w_ada': nrm(ks[8], (DEPTH, D, N_MOD * D), 0.5 * D ** -0.5),
        'b_ada': nrm(ks[9], (DEPTH, N_MOD * D), 0.02),
        'norm_pre': 1.0 + nrm(ks[10], (DEPTH, 3, D), 0.1),
        'norm_post': 1.0 + nrm(ks[11], (DEPTH, 3, D), 0.1),
        'ffn_w_in': nrm(ks[12], (DEPTH, 2, D, 2 * FFN_DIM), D ** -0.5),
        'ffn_w_out': nrm(ks[13], (DEPTH, 2, FFN_DIM, D), FFN_DIM ** -0.5),
        'w_in': nrm(ks[14], (DEPTH, D, IN_COLS), D ** -0.5),
        'na_rpb': nrm(ks[15], (DEPTH, NA_HEADS, 2 * NA_WIN_H - 1, 2 * NA_WIN_W - 1), 0.5),
        'pool_w': nrm(ks[16], (DEPTH, POOL_GROUPS, POOL_GROUP_DIM, POOL_GROUP_DIM), POOL_GROUP_DIM ** -0.5),
        'pool_scale': 1.0 + nrm(ks[17], (DEPTH, POOL_DIM), 0.1),
        'mla_q_norm': 1.0 + nrm(ks[18], (DEPTH, MLA_Q_LORA), 0.1),
        'mla_kv_norm': 1.0 + nrm(ks[19], (DEPTH, MLA_KV_LORA), 0.1),
        'mla_w_uq': nrm(ks[20], (DEPTH, MLA_Q_LORA, MLA_HEADS * (MLA_NOPE + MLA_ROPE)), MLA_Q_LORA ** -0.5),
        'mla_w_ukv': nrm(ks[21], (DEPTH, MLA_KV_LORA, MLA_HEADS * (MLA_NOPE + MLA_V)), MLA_KV_LORA ** -0.5),
        'w_branch': nrm(ks[22], (DEPTH, 3, BRANCH_W, D), BRANCH_W ** -0.5),
        'w_out': nrm(ks[23], (DEPTH, D, D), D ** -0.5),
    }


def reference(x_prompt, x_sample, cache_na_k, cache_na_v, cache_mla_ckv, cache_mla_krope, c, c_ctx,
              w_ada, b_ada, norm_pre, norm_post, ffn_w_in, ffn_w_out, w_in, na_rpb, pool_w, pool_scale,
              mla_q_norm, mla_kv_norm, mla_w_uq, mla_w_ukv, w_branch, w_out):
    xp, xs = x_prompt, x_sample
    st_k, st_v, st_ckv, st_kr = [], [], [], []
    for l in range(DEPTH):
        ffn_args = (norm_pre[l], norm_post[l], ffn_w_in[l], ffn_w_out[l])
        mods_ctx = modulations(c_ctx[None, :], w_ada[l], b_ada[l])
        mods_lat = modulations(c, w_ada[l], b_ada[l])
        xp, (k_l, v_l, ckv_l, kr_l) = trunk_layer(
            xp, mods_ctx,
            lambda u: context_mixer(u, w_in[l], pool_w[l], pool_scale[l], mla_q_norm[l], mla_kv_norm[l],
                                    mla_w_uq[l], mla_w_ukv[l], w_branch[l], w_out[l]),
            *ffn_args)
        xs, _ = trunk_layer(
            xs, mods_lat,
            lambda u: (latent_mixer(u, cache_na_k[:, l], cache_na_v[:, l], cache_mla_ckv[:, l], cache_mla_krope[:, l],
                                    w_in[l], na_rpb[l], pool_w[l], pool_scale[l], mla_q_norm[l], mla_kv_norm[l],
                                    mla_w_uq[l], mla_w_ukv[l], w_branch[l], w_out[l]), None),
            *ffn_args)
        st_k.append(k_l)
        st_v.append(v_l)
        st_ckv.append(ckv_l)
        st_kr.append(kr_l)
    state_na_k = jnp.stack(st_k, axis=1)
    state_na_v = jnp.stack(st_v, axis=1)
    state_mla_ckv = jnp.stack(st_ckv, axis=1)
    state_mla_krope = jnp.stack(st_kr, axis=1)
    return (xp, xs, state_na_k, state_na_v, state_mla_ckv, state_mla_krope)
```

```python
import functools

import numpy as np
import jax
import jax.numpy as jnp
from jax import lax
from jax.experimental import pallas as pl
from jax.experimental.pallas import tpu as pltpu

D_MODEL = 1024
DEPTH = 2
GRID_W = 64
N_MOD = 9
FFN_DIM = 2816
BRANCH_W = 512
NA_HEADS = 8
NA_HEAD_DIM = 64
NA_WIN_H = 8
NA_WIN_W = 16
POOL_GROUPS = 4
POOL_GROUP_DIM = 128
POOL_WINDOWS = (2, 4, 8, 16)
POOL_HALO = 8
MLA_HEADS = 8
MLA_Q_LORA = 256
MLA_KV_LORA = 256
MLA_NOPE = 64
MLA_ROPE = 32
MLA_V = 64
ROPE_BASE = 10000.0
NORM_EPS = 1e-6
NEG_INF = -1e30

LANES = 128
MLA_QK_PAD = LANES
VMEM_LIMIT_BYTES = 56 * 1024 * 1024

NA_QROWS = 4
NA_KROWS = 12

BF = jnp.bfloat16
F32 = jnp.float32

_NT = (((1,), (1,)), ((), ()))


def _dot(a, b):
    return jnp.dot(a, b, preferred_element_type=F32)


def _dot_nt(a, b):
    return lax.dot_general(a, b, _NT, preferred_element_type=F32)


def _sigmoid(x):
    return 1.0 / (1.0 + jnp.exp(-x))


def _silu(x):
    return x * _sigmoid(x)


def _rms(x, g):
    ms = jnp.mean(x * x, axis=-1, keepdims=True)
    return (x * lax.rsqrt(ms + NORM_EPS)) * g


def _const_spec(shape):
    nd = len(shape)
    return pl.BlockSpec(shape, lambda *_: (0,) * nd, pipeline_mode=pl.Buffered(1))


def _params(*sem):
    return pltpu.CompilerParams(dimension_semantics=sem, vmem_limit_bytes=VMEM_LIMIT_BYTES)


def _mods_kernel(c_ref, w_ref, b_ref, o_ref):
    a = _silu(c_ref[...]).astype(BF)
    o_ref[0] = _dot(a, w_ref[0].astype(BF)) + b_ref[0]


def _modulations(cvec, w_ada, b_ada):
    r = cvec.shape[0]
    ncol = N_MOD * D_MODEL
    tn = 1024
    out = pl.pallas_call(
        _mods_kernel,
        out_shape=jax.ShapeDtypeStruct((DEPTH, r, ncol), F32),
        grid=(DEPTH, ncol // tn),
        in_specs=[
            pl.BlockSpec((r, D_MODEL), lambda l, j: (0, 0)),
            pl.BlockSpec((1, D_MODEL, tn), lambda l, j: (l, 0, j)),
            pl.BlockSpec((1, 1, tn), lambda l, j: (l, 0, j)),
        ],
        out_specs=pl.BlockSpec((1, r, tn), lambda l, j: (l, 0, j)),
        compiler_params=_params("arbitrary", "arbitrary"),
        name="mods",
    )(cvec, w_ada, b_ada.reshape(DEPTH, 1, ncol))
    return out.reshape(DEPTH, r, N_MOD, D_MODEL)


_FFN_CHUNKS = ((0, 512), (512, 512), (1024, 512), (1536, 512), (2048, 512), (2560, 256))


def _ffn_kernel(x_ref, mod_ref, nrm_ref, win_ref, wout_ref, o_ref):
    x = x_ref[0]
    shift, scale, gate = mod_ref[0, 0:1, :], mod_ref[0, 1:2, :], mod_ref[0, 2:3, :]
    u = (_rms(x, nrm_ref[0:1, :]) * (1.0 + scale) + shift).astype(BF)
    acc = None
    for off, ck in _FFN_CHUNKS:
        g = _dot(u, win_ref[:, off:off + ck])
        up = _dot(u, win_ref[:, FFN_DIM + off:FFN_DIM + off + ck])
        d = _dot((_silu(g) * up).astype(BF), wout_ref[off:off + ck, :])
        acc = d if acc is None else acc + d
    o_ref[0] = x + 0.5 * (gate * _rms(acc, nrm_ref[1:2, :]))


def _ffn(x, mods3, nrm2, w_in, w_out, tm):
    b, n, _ = x.shape
    return pl.pallas_call(
        _ffn_kernel,
        out_shape=jax.ShapeDtypeStruct(x.shape, F32),
        grid=(b, n // tm),
        in_specs=[
            pl.BlockSpec((1, tm, D_MODEL), lambda i, j: (i, j, 0)),
            pl.BlockSpec((1, 3, D_MODEL), lambda i, j: (i, 0, 0)),
            _const_spec((2, D_MODEL)),
            _const_spec((D_MODEL, 2 * FFN_DIM)),
            _const_spec((FFN_DIM, D_MODEL)),
        ],
        out_specs=pl.BlockSpec((1, tm, D_MODEL), lambda i, j: (i, j, 0)),
        compiler_params=_params("arbitrary", "arbitrary"),
        name="ffn",
    )(x, mods3, nrm2, w_in, w_out)


_C_Q, _C_K, _C_V, _C_POOL, _C_CQ, _C_CKV, _C_KR, _C_KRS, _C_END = (
    0, 512, 1024, 1536, 2048, 2304, 2560, 2688, 2816)


def _inproj_kernel(*refs, latent):
    if latent:
        (x_ref, mod_ref, g_ref, w1_ref, qn_ref, kvn_ref, wuq_ref, wuqs_ref, wuk_ref, wuv_ref,
         cos_ref, sin_ref,
         q_ref, k_ref, v_ref, pool_ref, qm_ref, km_ref, vm_ref) = refs
    else:
        (x_ref, mod_ref, g_ref, w1_ref, qn_ref, kvn_ref, wuq_ref, wuqs_ref, wuk_ref, wuv_ref,
         q_ref, k_ref, v_ref, pool_ref, qm_ref, km_ref, vm_ref, ckv_ref, kr_ref) = refs
    x = x_ref[0]
    shift, scale = mod_ref[0, 0:1, :], mod_ref[0, 1:2, :]
    u = (_rms(x, g_ref[...]) * (1.0 + scale) + shift).astype(BF)
    h = _dot(u, w1_ref[...])
    q_ref[0] = (h[:, _C_Q:_C_K] * (NA_HEAD_DIM ** -0.5)).astype(q_ref.dtype)
    k_ref[0] = h[:, _C_K:_C_V].astype(k_ref.dtype)
    v_ref[0] = h[:, _C_V:_C_POOL].astype(v_ref.dtype)
    pool_ref[0] = h[:, _C_POOL:_C_CQ]

    qn = _rms(h[:, _C_CQ:_C_CKV], qn_ref[...]).astype(BF)
    ckv = _rms(h[:, _C_CKV:_C_KR], kvn_ref[...])
    ckv_b = ckv.astype(BF)
    kr = h[:, _C_KR:_C_KRS]
    qa = _dot(qn, wuq_ref[...])
    if latent:
        cos, sin = cos_ref[...], sin_ref[...]
        qb = _dot(qn, wuqs_ref[...])
        kr = kr * cos + h[:, _C_KRS:_C_END] * sin
    else:
        ckv_ref[0] = ckv
        kr_ref[0] = kr
    kn = _dot(ckv_b, wuk_ref[...])
    vm_ref[0] = _dot(ckv_b, wuv_ref[...]).astype(BF)
    qscale = (MLA_NOPE + MLA_ROPE) ** -0.5
    for hd in range(MLA_HEADS):
        sl = slice(hd * MLA_QK_PAD, (hd + 1) * MLA_QK_PAD)
        qh = qa[:, sl]
        if latent:
            qh = qh * cos + qb[:, sl] * sin
        qm_ref[0, :, sl] = (qh * qscale).astype(BF)
        km_ref[0, :, sl] = (kn[:, sl] + kr).astype(BF)


def _inproj(x, mods3, g_pre, wts, rope, tm, latent):
    b, n, _ = x.shape
    tok = lambda w, dt: jax.ShapeDtypeStruct((b, n, w), dt)
    tspec = lambda w: pl.BlockSpec((1, tm, w), lambda i, j: (i, j, 0))
    kv_dt = BF if latent else F32
    out_shape = [tok(512, BF), tok(512, kv_dt), tok(512, kv_dt), tok(512, F32),
                 tok(1024, BF), tok(1024, BF), tok(512, BF)]
    out_specs = [tspec(512), tspec(512), tspec(512), tspec(512), tspec(1024), tspec(1024), tspec(512)]
    in_specs = [
        tspec(D_MODEL),
        pl.BlockSpec((1, 3, D_MODEL), lambda i, j: (i, 0, 0)),
        _const_spec((1, D_MODEL)),
        _const_spec((D_MODEL, _C_END)),
        _const_spec((1, MLA_Q_LORA)),
        _const_spec((1, MLA_KV_LORA)),
        _const_spec((MLA_Q_LORA, MLA_HEADS * MLA_QK_PAD)),
        _const_spec((MLA_Q_LORA, MLA_HEADS * MLA_QK_PAD)),
        _const_spec((MLA_KV_LORA, MLA_HEADS * MLA_QK_PAD)),
        _const_spec((MLA_KV_LORA, MLA_HEADS * MLA_V)),
    ]
    args = [x, mods3, g_pre, wts["w1"], wts["q_norm"], wts["kv_norm"], wts["w_uq"], wts["w_uq_sw"],
            wts["w_uk"], wts["w_uv"]]
    if latent:
        in_specs += [pl.BlockSpec((tm, LANES), lambda i, j: (j, 0))] * 2
        args += [rope[0], rope[1]]
    else:
        out_shape += [tok(MLA_KV_LORA, F32), tok(LANES, F32)]
        out_specs += [tspec(MLA_KV_LORA), tspec(LANES)]
    return pl.pallas_call(
        functools.partial(_inproj_kernel, latent=latent),
        out_shape=out_shape,
        grid=(b, n // tm),
        in_specs=in_specs,
        out_specs=out_specs,
        compiler_params=_params("arbitrary", "arbitrary"),
        name="inproj_lat" if latent else "inproj_ctx",
    )(*args)


def _cache_kv_kernel(ckv_ref, kr_ref, wuk_ref, wuv_ref, km_ref, vm_ref):
    ckv = ckv_ref[0, 0].astype(BF)
    kr = kr_ref[0, 0]
    kn = _dot(ckv, wuk_ref[0])
    vm_ref[0, 0] = _dot(ckv, wuv_ref[0]).astype(BF)
    for hd in range(MLA_HEADS):
        sl = slice(hd * MLA_QK_PAD, (hd + 1) * MLA_QK_PAD)
        km_ref[0, 0, :, sl] = (kn[:, sl] + kr).astype(BF)


def _cache_kv(ckv_c, kr_pad_c, w_uk, w_uv):
    _, b, p, _ = ckv_c.shape
    spec = lambda w: pl.BlockSpec((1, 1, p, w), lambda l, i: (l, i, 0, 0))
    wspec = lambda r, w: pl.BlockSpec((1, r, w), lambda l, i: (l, 0, 0))
    return pl.pallas_call(
        _cache_kv_kernel,
        out_shape=[jax.ShapeDtypeStruct((DEPTH, b, p, MLA_HEADS * MLA_QK_PAD), BF),
                   jax.ShapeDtypeStruct((DEPTH, b, p, MLA_HEADS * MLA_V), BF)],
        grid=(DEPTH, b),
        in_specs=[spec(MLA_KV_LORA), spec(LANES),
                  wspec(MLA_KV_LORA, MLA_HEADS * MLA_QK_PAD), wspec(MLA_KV_LORA, MLA_HEADS * MLA_V)],
        out_specs=[spec(MLA_HEADS * MLA_QK_PAD), spec(MLA_HEADS * MLA_V)],
        compiler_params=_params("arbitrary", "arbitrary"),
        name="cache_kv",
    )(ckv_c, kr_pad_c, w_uk, w_uv)


def _half_mask(width, j):
    lane = lax.broadcasted_iota(jnp.int32, (1, width), 1)
    return (lane // NA_HEAD_DIM) == j


def _softmax_pv(scores, values):
    m = None
    for s in scores:
        mi = jnp.max(s, axis=-1, keepdims=True)
        m = mi if m is None else jnp.maximum(m, mi)
    l = None
    o = None
    for s, v in zip(scores, values):
        p = jnp.exp(s - m)
        li = jnp.sum(p, axis=-1, keepdims=True)
        oi = _dot(p.astype(BF), v)
        l = li if l is None else l + li
        o = oi if o is None else o + oi
    return o * (1.0 / l)


def _ctx_attn_kernel(q_ref, k_ref, v_ref, qm_ref, km_ref, vm_ref, na_ref, mla_ref):
    zero = jnp.zeros((), BF)
    for hp in range(NA_HEADS // 2):
        sl = slice(hp * LANES, (hp + 1) * LANES)
        qblk = q_ref[0, :, sl]
        kblk = k_ref[0, :, sl].astype(BF)
        vblk = v_ref[0, :, sl].astype(BF)
        vmblk = vm_ref[0, :, sl]
        acc_na = None
        acc_mla = None
        for j in range(2):
            hm = _half_mask(LANES, j)
            s = _dot_nt(jnp.where(hm, qblk, zero), kblk)
            o = _softmax_pv([s], [jnp.where(hm, vblk, zero)])
            acc_na = o if acc_na is None else acc_na + o
            hd = 2 * hp + j
            hsl = slice(hd * MLA_QK_PAD, (hd + 1) * MLA_QK_PAD)
            s = _dot_nt(qm_ref[0, :, hsl], km_ref[0, :, hsl])
            o = _softmax_pv([s], [jnp.where(hm, vmblk, zero)])
            acc_mla = o if acc_mla is None else acc_mla + o
        na_ref[0, :, sl] = acc_na.astype(BF)
        mla_ref[0, :, sl] = acc_mla.astype(BF)


def _ctx_attn(q, k, v, qm, km, vm):
    b, s, _ = q.shape
    spec = lambda w: pl.BlockSpec((1, s, w), lambda i: (i, 0, 0))
    return pl.pallas_call(
        _ctx_attn_kernel,
        out_shape=[jax.ShapeDtypeStruct((b, s, BRANCH_W), BF)] * 2,
        grid=(b,),
        in_specs=[spec(512), spec(512), spec(512), spec(1024), spec(1024), spec(512)],
        out_specs=[spec(BRANCH_W), spec(BRANCH_W)],
        compiler_params=_params("arbitrary"),
        name="ctx_attn",
    )(q, k, v, qm, km, vm)


def _lat_mla_kernel(q_ref, k_ref, v_ref, kc_ref, vc_ref, o_ref):
    zero = jnp.zeros((), BF)
    for hp in range(MLA_HEADS // 2):
        sl = slice(hp * LANES, (hp + 1) * LANES)
        vblk = v_ref[0, :, sl]
        vcblk = vc_ref[0, :, sl]
        acc = None
        for j in range(2):
            hm = _half_mask(LANES, j)
            hd = 2 * hp + j
            hsl = slice(hd * MLA_QK_PAD, (hd + 1) * MLA_QK_PAD)
            q = q_ref[0, :, hsl]
            s = _dot_nt(q, k_ref[0, :, hsl])
            sc = _dot_nt(q, kc_ref[0, :, hsl])
            o = _softmax_pv([s, sc], [jnp.where(hm, vblk, zero), jnp.where(hm, vcblk, zero)])
            acc = o if acc is None else acc + o
        o_ref[0, :, sl] = acc.astype(BF)


def _lat_mla(qm, km, vm, kc, vc, tq):
    b, n, _ = qm.shape
    p = kc.shape[1]
    return pl.pallas_call(
        _lat_mla_kernel,
        out_shape=jax.ShapeDtypeStruct((b, n, BRANCH_W), BF),
        grid=(b, n // tq),
        in_specs=[
            pl.BlockSpec((1, tq, 1024), lambda i, j: (i, j, 0)),
            pl.BlockSpec((1, n, 1024), lambda i, j: (i, 0, 0)),
            pl.BlockSpec((1, n, 512), lambda i, j: (i, 0, 0)),
            pl.BlockSpec((1, p, 1024), lambda i, j: (i, 0, 0)),
            pl.BlockSpec((1, p, 512), lambda i, j: (i, 0, 0)),
        ],
        out_specs=pl.BlockSpec((1, tq, BRANCH_W), lambda i, j: (i, j, 0)),
        compiler_params=_params("arbitrary", "arbitrary"),
        name="lat_mla",
    )(qm, km, vm, kc, vc)


def _na_bias_tables(rpb, rows):
    tables = []
    for r0 in (0, 2 * NA_QROWS, rows - NA_QROWS):
        ks = int(np.clip(r0 - NA_WIN_H // 2, 0, rows - NA_KROWS))
        r = r0 + np.arange(NA_QROWS)[:, None, None, None]
        qc = np.arange(GRID_W)[None, :, None, None]
        kr = ks + np.arange(NA_KROWS)[None, None, :, None]
        kc = np.arange(GRID_W)[None, None, None, :]
        rs = np.clip(r - NA_WIN_H // 2, 0, rows - NA_WIN_H)
        ws = np.clip(qc - NA_WIN_W // 2, 0, GRID_W - NA_WIN_W)
        valid = (kr >= rs) & (kr < rs + NA_WIN_H) & (kc >= ws) & (kc < ws + NA_WIN_W)
        ridx = np.clip(kr - r + NA_WIN_H - 1, 0, 2 * NA_WIN_H - 2)
        cidx = np.clip(kc - qc + NA_WIN_W - 1, 0, 2 * NA_WIN_W - 2)
        shape = (NA_QROWS, GRID_W, NA_KROWS, GRID_W)
        ridx, cidx, valid = (np.broadcast_to(a, shape).reshape(NA_QROWS * GRID_W, NA_KROWS * GRID_W)
                             for a in (ridx, cidx, valid))
        tables.append(jnp.where(valid[None], rpb[:, ridx, cidx], NEG_INF))
    return jnp.stack(tables)


def _lat_na_kernel(q_ref, k_ref, v_ref, kc_ref, vc_ref, bias_ref, o_ref, *, rows):
    g = pl.program_id(1)
    ks = jnp.clip(NA_QROWS * g - NA_WIN_H // 2, 0, rows - NA_KROWS)
    k0 = pl.multiple_of(ks * GRID_W, GRID_W)
    nk = NA_KROWS * GRID_W
    zero = jnp.zeros((), BF)
    for hp in range(NA_HEADS // 2):
        sl = slice(hp * LANES, (hp + 1) * LANES)
        qblk = q_ref[0, :, sl]
        kblk = k_ref[0, pl.ds(k0, nk), sl]
        vblk = v_ref[0, pl.ds(k0, nk), sl]
        kcblk = kc_ref[0, :, sl]
        vcblk = vc_ref[0, :, sl]
        acc = None
        for j in range(2):
            hm = _half_mask(LANES, j)
            qh = jnp.where(hm, qblk, zero)
            s = _dot_nt(qh, kblk) + bias_ref[0, 2 * hp + j]
            sc = _dot_nt(qh, kcblk)
            o = _softmax_pv([s, sc], [jnp.where(hm, vblk, zero), jnp.where(hm, vcblk, zero)])
            acc = o if acc is None else acc + o
        o_ref[0, :, sl] = acc.astype(BF)


def _lat_na(q, k, v, kc, vc, bias):
    b, n, _ = q.shape
    p = kc.shape[1]
    rows = n // GRID_W
    ngroups = rows // NA_QROWS
    tq = NA_QROWS * GRID_W
    nk = NA_KROWS * GRID_W

    def bias_map(i, g):
        kind = jnp.where(g == 0, 0, jnp.where(g == ngroups - 1, 2, 1))
        return (kind, 0, 0, 0)

    return pl.pallas_call(
        functools.partial(_lat_na_kernel, rows=rows),
        out_shape=jax.ShapeDtypeStruct((b, n, BRANCH_W), BF),
        grid=(b, ngroups),
        in_specs=[
            pl.BlockSpec((1, tq, 512), lambda i, g: (i, g, 0)),
            pl.BlockSpec((1, n, 512), lambda i, g: (i, 0, 0)),
            pl.BlockSpec((1, n, 512), lambda i, g: (i, 0, 0)),
            pl.BlockSpec((1, p, 512), lambda i, g: (i, 0, 0)),
            pl.BlockSpec((1, p, 512), lambda i, g: (i, 0, 0)),
            pl.BlockSpec((1, NA_HEADS, tq, nk), bias_map),
        ],
        out_specs=pl.BlockSpec((1, tq, BRANCH_W), lambda i, g: (i, g, 0)),
        compiler_params=_params("arbitrary", "arbitrary"),
        name="lat_na",
    )(q, k, v, kc, vc, bias)


def _merge_kernel(x_ref, mod_ref, nrm_ref, na_ref, mla_ref, pool_ref, prev_ref, next_ref,
                  wg_ref, wb_ref, wo_ref, pw_ref, ps_ref, o_ref, ext_ref, *, seq_len):
    tm = x_ref.shape[1]
    t = pl.program_id(1)
    nt = pl.num_programs(1)
    x = x_ref[0]
    shift, scale, gate = mod_ref[0, 0:1, :], mod_ref[0, 1:2, :], mod_ref[0, 2:3, :]
    u = (_rms(x, nrm_ref[0:1, :]) * (1.0 + scale) + shift).astype(BF)

    cur = pool_ref[0]
    ext_ref[0:POOL_HALO, :] = jnp.where(t > 0, prev_ref[0], 0.0)
    ext_ref[POOL_HALO:POOL_HALO + tm, :] = cur
    ext_ref[POOL_HALO + tm:POOL_HALO + tm + POOL_HALO, :] = jnp.where(t < nt - 1, next_ref[0], 0.0)
    pos = t * tm + lax.broadcasted_iota(jnp.int32, (tm, 1), 0)
    pooled = []
    for gi, w in enumerate(POOL_WINDOWS):
        sl = slice(gi * POOL_GROUP_DIM, (gi + 1) * POOL_GROUP_DIM)
        tot = None
        for j in range(-(w // 2), w // 2):
            piece = ext_ref[POOL_HALO + j:POOL_HALO + j + tm, sl]
            tot = piece if tot is None else tot + piece
        cnt = (jnp.minimum(pos + w // 2, seq_len) - jnp.maximum(pos - w // 2, 0)).astype(F32)
        diff = (tot / cnt - cur[:, sl]).astype(BF)
        pooled.append(_dot(diff, pw_ref[gi]))
    pool_o = (jnp.concatenate(pooled, axis=-1) * ps_ref[...]).astype(BF)

    branches = (na_ref[0], pool_o, mla_ref[0])
    z = None
    for kb in range(3):
        y = _dot(branches[kb], wb_ref[kb])
        gk = _sigmoid(_dot(u, wg_ref[:, kb * D_MODEL:(kb + 1) * D_MODEL]))
        z = gk * y if z is None else z + gk * y
    y = _dot(z.astype(BF), wo_ref[...])
    o_ref[0] = x + gate * _rms(y, nrm_ref[1:2, :])


def _merge(x, mods3, nrm2, na_o, mla_o, pool_in, wts, tm):
    b, n, _ = x.shape
    nh = tm // POOL_HALO
    last_h = n // POOL_HALO - 1
    tspec = lambda w: pl.BlockSpec((1, tm, w), lambda i, j: (i, j, 0))
    return pl.pallas_call(
        functools.partial(_merge_kernel, seq_len=n),
        out_shape=jax.ShapeDtypeStruct(x.shape, F32),
        grid=(b, n // tm),
        in_specs=[
            tspec(D_MODEL),
            pl.BlockSpec((1, 3, D_MODEL), lambda i, j: (i, 0, 0)),
            _const_spec((2, D_MODEL)),
            tspec(BRANCH_W), tspec(BRANCH_W), tspec(512),
            pl.BlockSpec((1, POOL_HALO, 512), lambda i, j: (i, jnp.maximum(j * nh - 1, 0), 0)),
            pl.BlockSpec((1, POOL_HALO, 512), lambda i, j: (i, jnp.minimum((j + 1) * nh, last_h), 0)),
            _const_spec((D_MODEL, 3 * D_MODEL)),
            _const_spec((3, BRANCH_W, D_MODEL)),
            _const_spec((D_MODEL, D_MODEL)),
            _const_spec((POOL_GROUPS, POOL_GROUP_DIM, POOL_GROUP_DIM)),
            _const_spec((1, 512)),
        ],
        out_specs=tspec(D_MODEL),
        scratch_shapes=[pltpu.VMEM((tm + 2 * POOL_HALO, 512), F32)],
        compiler_params=_params("arbitrary", "arbitrary"),
        name="merge",
    )(x, mods3, nrm2, na_o, mla_o, pool_in, pool_in, pool_in,
      wts["w_g"], wts["w_branch"], wts["w_out"], wts["pool_w"], wts["pool_scale"])


def _rope_tables(n_tok):
    half = MLA_ROPE // 2
    n_freq = half // 2
    inv = ROPE_BASE ** (-jnp.arange(n_freq, dtype=F32) / n_freq)
    t = jnp.arange(n_tok)
    ang_r = (t // GRID_W).astype(F32)[:, None] * inv
    ang_c = (t % GRID_W).astype(F32)[:, None] * inv
    ones = jnp.ones((n_tok, MLA_NOPE), F32)
    tail = jnp.zeros((n_tok, MLA_QK_PAD - MLA_NOPE - MLA_ROPE), F32)
    cos = jnp.concatenate([ones, jnp.cos(ang_r), jnp.cos(ang_r), jnp.cos(ang_c), jnp.cos(ang_c), tail + 1.0], axis=-1)
    sin = jnp.concatenate([0.0 * ones, -jnp.sin(ang_r), jnp.sin(ang_r), -jnp.sin(ang_c), jnp.sin(ang_c), tail], axis=-1)
    return cos, sin


def _rope_partner_perm():
    n_freq = MLA_ROPE // 4
    idx = np.arange(MLA_ROPE)
    return np.where((idx // n_freq) % 2 == 0, idx + n_freq, idx - n_freq)


def _pad_heads(w, parts):
    r, h, _ = w.shape
    cols = jnp.concatenate(parts, axis=-1)
    pad = MLA_QK_PAD - cols.shape[-1]
    return jnp.pad(cols, ((0, 0), (0, 0), (0, pad))).reshape(r, h * MLA_QK_PAD)


def _layer_weights(l, w_in, pool_w, pool_scale, q_norm, kv_norm, w_uq, w_ukv, w_branch, w_out):
    perm = _rope_partner_perm()
    bounds = np.cumsum([512, 512, 512, 512, MLA_Q_LORA, MLA_KV_LORA, MLA_ROPE])
    wi = w_in[l]
    kr_cols = wi[:, bounds[5]:bounds[6]]
    lane_pad = lambda c: jnp.pad(c, ((0, 0), (MLA_NOPE, MLA_QK_PAD - MLA_NOPE - MLA_ROPE)))
    w1 = jnp.concatenate([wi[:, :bounds[5]], lane_pad(kr_cols), lane_pad(kr_cols[:, perm])], axis=-1).astype(BF)
    uq = w_uq[l].reshape(MLA_Q_LORA, MLA_HEADS, MLA_NOPE + MLA_ROPE)
    ukv = w_ukv[l].reshape(MLA_KV_LORA, MLA_HEADS, MLA_NOPE + MLA_V)
    zeros_nope = jnp.zeros((MLA_Q_LORA, MLA_HEADS, MLA_NOPE), F32)
    return {
        "w1": w1,
        "w_g": wi[:, bounds[6]:].astype(BF),
        "q_norm": q_norm[l][None, :],
        "kv_norm": kv_norm[l][None, :],
        "w_uq": _pad_heads(uq, [uq]).astype(BF),
        "w_uq_sw": _pad_heads(uq, [zeros_nope, uq[..., MLA_NOPE:][..., perm]]).astype(BF),
        "w_uk": _pad_heads(ukv, [ukv[..., :MLA_NOPE]]).astype(BF),
        "w_uv": ukv[..., MLA_NOPE:].reshape(MLA_KV_LORA, MLA_HEADS * MLA_V).astype(BF),
        "w_branch": w_branch[l].astype(BF),
        "w_out": w_out[l].astype(BF),
        "pool_w": pool_w[l].astype(BF),
        "pool_scale": pool_scale[l][None, :],
    }


def kernel(x_prompt, x_sample, cache_na_k, cache_na_v, cache_mla_ckv, cache_mla_krope, c, c_ctx, w_ada, b_ada, norm_pre, norm_post, ffn_w_in, ffn_w_out, w_in, na_rpb, pool_w, pool_scale, mla_q_norm, mla_kv_norm, mla_w_uq, mla_w_ukv, w_branch, w_out):
    batch, seq, _ = x_prompt.shape
    dec_batch, dec_seq, _ = x_sample.shape
    past = cache_na_k.shape[2]
    rows = dec_seq // GRID_W

    n_rows = -(-(1 + dec_batch) // 8) * 8
    cvec = jnp.concatenate([c_ctx[None, :], c, jnp.zeros((n_rows - 1 - dec_batch, D_MODEL), F32)], axis=0)
    mods = _modulations(cvec, w_ada, b_ada)

    lw = [_layer_weights(l, w_in, pool_w, pool_scale, mla_q_norm, mla_kv_norm, mla_w_uq, mla_w_ukv,
                         w_branch, w_out) for l in range(DEPTH)]
    rope = _rope_tables(dec_seq)
    ffn_in = ffn_w_in.astype(BF)
    ffn_out = ffn_w_out.astype(BF)

    kc_na = cache_na_k.reshape(dec_batch, DEPTH, past, 512).transpose(1, 0, 2, 3).astype(BF)
    vc_na = cache_na_v.reshape(dec_batch, DEPTH, past, 512).transpose(1, 0, 2, 3).astype(BF)
    ckv_c = cache_mla_ckv.transpose(1, 0, 2, 3)
    kr_c = jnp.pad(cache_mla_krope.transpose(1, 0, 2, 3),
                   ((0, 0), (0, 0), (0, 0), (MLA_NOPE, MLA_QK_PAD - MLA_NOPE - MLA_ROPE)))
    kc_mla, vc_mla = _cache_kv(ckv_c, kr_c, jnp.stack([w["w_uk"] for w in lw]), jnp.stack([w["w_uv"] for w in lw]))

    xp = x_prompt.reshape(1, batch * seq, D_MODEL)
    xs = x_sample
    states = [[], [], [], []]
    tm = 512
    for l in range(DEPTH):
        m_ctx, m_lat = mods[l, 0:1], mods[l, 1:1 + dec_batch]
        nrm = lambda j: jnp.stack([norm_pre[l, j], norm_post[l, j]])
        w = lw[l]
        bias = _na_bias_tables(na_rpb[l], rows)

        xp = _ffn(xp, m_ctx[:, 0:3], nrm(0), ffn_in[l, 0], ffn_out[l, 0], tm)
        xs = _ffn(xs, m_lat[:, 0:3], nrm(0), ffn_in[l, 0], ffn_out[l, 0], tm)

        q, k, v, pool_in, qm, km, vm, ckv, kr = _inproj(
            xp, m_ctx[:, 3:6], norm_pre[l, 1][None, :], w, None, tm, latent=False)
        per_seq = lambda a: a.reshape(batch, seq, a.shape[-1])
        na_o, mla_o = _ctx_attn(*(per_seq(a) for a in (q, k, v, qm, km, vm)))
        xp = _merge(xp.reshape(batch, seq, D_MODEL), jnp.broadcast_to(m_ctx[:, 3:6], (batch, 3, D_MODEL)),
                    nrm(1), na_o, mla_o, per_seq(pool_in), w, seq).reshape(1, batch * seq, D_MODEL)
        states[0].append(k.reshape(batch, seq, NA_HEADS, NA_HEAD_DIM))
        states[1].append(v.reshape(batch, seq, NA_HEADS, NA_HEAD_DIM))
        states[2].append(ckv.reshape(batch, seq, MLA_KV_LORA))
        states[3].append(kr.reshape(batch, seq, LANES)[..., MLA_NOPE:MLA_NOPE + MLA_ROPE])

        q, k, v, pool_in, qm, km, vm = _inproj(xs, m_lat[:, 3:6], norm_pre[l, 1][None, :], w, rope, tm, latent=True)
        na_o = _lat_na(q, k, v, kc_na[l], vc_na[l], bias)
        mla_o = _lat_mla(qm, km, vm, kc_mla[l], vc_mla[l], 256)
        xs = _merge(xs, m_lat[:, 3:6], nrm(1), na_o, mla_o, pool_in, w, tm)

        xp = _ffn(xp, m_ctx[:, 6:9], nrm(2), ffn_in[l, 1], ffn_out[l, 1], tm)
        xs = _ffn(xs, m_lat[:, 6:9], nrm(2), ffn_in[l, 1], ffn_out[l, 1], tm)

    return (xp.reshape(batch, seq, D_MODEL), xs) + tuple(jnp.stack(s, axis=1) for s in states)
```

```python
import functools

import numpy as np
import jax
import jax.numpy as jnp
from jax import lax
from jax.experimental import pallas as pl
from jax.experimental.pallas import tpu as pltpu

D_MODEL = 1024
DEPTH = 2
GRID_W = 64
N_MOD = 9
FFN_DIM = 2816
BRANCH_W = 512
NA_HEADS = 8
NA_HEAD_DIM = 64
NA_WIN_H = 8
NA_WIN_W = 16
POOL_GROUPS = 4
POOL_GROUP_DIM = 128
POOL_WINDOWS = (2, 4, 8, 16)
POOL_HALO = 8
MLA_HEADS = 8
MLA_Q_LORA = 256
MLA_KV_LORA = 256
MLA_NOPE = 64
MLA_ROPE = 32
MLA_V = 64
ROPE_BASE = 10000.0
NORM_EPS = 1e-6
NEG_INF = -1e30

LANES = 128
MLA_QK_PAD = LANES
VMEM_LIMIT_BYTES = 56 * 1024 * 1024

NA_QROWS = 4
NA_KROWS = 12

BF = jnp.bfloat16
F32 = jnp.float32

_NT = (((1,), (1,)), ((), ()))


def _dot(a, b):
    return jnp.dot(a, b, preferred_element_type=F32)


def _dot_nt(a, b):
    return lax.dot_general(a, b, _NT, preferred_element_type=F32)


def _sigmoid(x):
    return 1.0 / (1.0 + jnp.exp(-x))


def _silu(x):
    return x * _sigmoid(x)


def _rms(x, g):
    ms = jnp.mean(x * x, axis=-1, keepdims=True)
    return (x * lax.rsqrt(ms + NORM_EPS)) * g


def _const_spec(shape):
    nd = len(shape)
    return pl.BlockSpec(shape, lambda *_: (0,) * nd, pipeline_mode=pl.Buffered(1))


def _params(*sem):
    return pltpu.CompilerParams(dimension_semantics=sem, vmem_limit_bytes=VMEM_LIMIT_BYTES)


def _mods_kernel(c_ref, w_ref, b_ref, o_ref):
    a = _silu(c_ref[...]).astype(BF)
    o_ref[0] = _dot(a, w_ref[0].astype(BF)) + b_ref[0]


def _modulations(cvec, w_ada, b_ada):
    r = cvec.shape[0]
    ncol = N_MOD * D_MODEL
    tn = 1024
    out = pl.pallas_call(
        _mods_kernel,
        out_shape=jax.ShapeDtypeStruct((DEPTH, r, ncol), F32),
        grid=(DEPTH, ncol // tn),
        in_specs=[
            pl.BlockSpec((r, D_MODEL), lambda l, j: (0, 0)),
            pl.BlockSpec((1, D_MODEL, tn), lambda l, j: (l, 0, j)),
            pl.BlockSpec((1, 1, tn), lambda l, j: (l, 0, j)),
        ],
        out_specs=pl.BlockSpec((1, r, tn), lambda l, j: (l, 0, j)),
        compiler_params=_params("arbitrary", "arbitrary"),
        name="mods",
    )(cvec, w_ada, b_ada.reshape(DEPTH, 1, ncol))
    return out.reshape(DEPTH, r, N_MOD, D_MODEL)


_FFN_CHUNKS = ((0, 512), (512, 512), (1024, 512), (1536, 512), (2048, 512), (2560, 256))


def _ffn_kernel(x_ref, mod_ref, nrm_ref, win_ref, wout_ref, o_ref):
    x = x_ref[0]
    shift, scale, gate = mod_ref[0, 0:1, :], mod_ref[0, 1:2, :], mod_ref[0, 2:3, :]
    u = (_rms(x, nrm_ref[0:1, :]) * (1.0 + scale) + shift).astype(BF)
    acc = None
    for off, ck in _FFN_CHUNKS:
        g = _dot(u, win_ref[:, off:off + ck])
        up = _dot(u, win_ref[:, FFN_DIM + off:FFN_DIM + off + ck])
        d = _dot((_silu(g) * up).astype(BF), wout_ref[off:off + ck, :])
        acc = d if acc is None else acc + d
    o_ref[0] = x + 0.5 * (gate * _rms(acc, nrm_ref[1:2, :]))


def _ffn(x, mods3, nrm2, w_in, w_out, tm):
    b, n, _ = x.shape
    return pl.pallas_call(
        _ffn_kernel,
        out_shape=jax.ShapeDtypeStruct(x.shape, F32),
        grid=(b, n // tm),
        in_specs=[
            pl.BlockSpec((1, tm, D_MODEL), lambda i, j: (i, j, 0)),
            pl.BlockSpec((1, 3, D_MODEL), lambda i, j: (i, 0, 0)),
            _const_spec((2, D_MODEL)),
            _const_spec((D_MODEL, 2 * FFN_DIM)),
            _const_spec((FFN_DIM, D_MODEL)),
        ],
        out_specs=pl.BlockSpec((1, tm, D_MODEL), lambda i, j: (i, j, 0)),
        compiler_params=_params("arbitrary", "arbitrary"),
        name="ffn",
    )(x, mods3, nrm2, w_in, w_out)


_C_Q, _C_K, _C_V, _C_POOL, _C_CQ, _C_CKV, _C_KR, _C_KRS, _C_END = (
    0, 512, 1024, 1536, 2048, 2304, 2560, 2688, 2816)


def _inproj_kernel(*refs, latent):
    if latent:
        (x_ref, mod_ref, g_ref, w1_ref, qn_ref, kvn_ref, wuq_ref, wuqs_ref, wuk_ref, wuv_ref,
         cos_ref, sin_ref,
         q_ref, k_ref, v_ref, pool_ref, qm_ref, km_ref, vm_ref) = refs
    else:
        (x_ref, mod_ref, g_ref, w1_ref, qn_ref, kvn_ref, wuq_ref, wuqs_ref, wuk_ref, wuv_ref,
         q_ref, k_ref, v_ref, pool_ref, qm_ref, km_ref, vm_ref, ckv_ref, kr_ref) = refs
    x = x_ref[0]
    shift, scale = mod_ref[0, 0:1, :], mod_ref[0, 1:2, :]
    u = (_rms(x, g_ref[...]) * (1.0 + scale) + shift).astype(BF)
    h = _dot(u, w1_ref[...])
    q_ref[0] = (h[:, _C_Q:_C_K] * (NA_HEAD_DIM ** -0.5)).astype(q_ref.dtype)
    k_ref[0] = h[:, _C_K:_C_V].astype(k_ref.dtype)
    v_ref[0] = h[:, _C_V:_C_POOL].astype(v_ref.dtype)
    pool_ref[0] = h[:, _C_POOL:_C_CQ]

    qn = _rms(h[:, _C_CQ:_C_CKV], qn_ref[...]).astype(BF)
    ckv = _rms(h[:, _C_CKV:_C_KR], kvn_ref[...])
    ckv_b = ckv.astype(BF)
    kr = h[:, _C_KR:_C_KRS]
    qa = _dot(qn, wuq_ref[...])
    if latent:
        cos, sin = cos_ref[...], sin_ref[...]
        qb = _dot(qn, wuqs_ref[...])
        kr = kr * cos + h[:, _C_KRS:_C_END] * sin
    else:
        ckv_ref[0] = ckv
        kr_ref[0] = kr
    kn = _dot(ckv_b, wuk_ref[...])
    vm_ref[0] = _dot(ckv_b, wuv_ref[...]).astype(BF)
    qscale = (MLA_NOPE + MLA_ROPE) ** -0.5
    for hd in range(MLA_HEADS):
        sl = slice(hd * MLA_QK_PAD, (hd + 1) * MLA_QK_PAD)
        qh = qa[:, sl]
        if latent:
            qh = qh * cos + qb[:, sl] * sin
        qm_ref[0, :, sl] = (qh * qscale).astype(BF)
        km_ref[0, :, sl] = (kn[:, sl] + kr).astype(BF)


def _inproj(x, mods3, g_pre, wts, rope, tm, latent):
    b, n, _ = x.shape
    tok = lambda w, dt: jax.ShapeDtypeStruct((b, n, w), dt)
    tspec = lambda w: pl.BlockSpec((1, tm, w), lambda i, j: (i, j, 0))
    kv_dt = BF if latent else F32
    out_shape = [tok(512, BF), tok(512, kv_dt), tok(512, kv_dt), tok(512, F32),
                 tok(1024, BF), tok(1024, BF), tok(512, BF)]
    out_specs = [tspec(512), tspec(512), tspec(512), tspec(512), tspec(1024), tspec(1024), tspec(512)]
    in_specs = [
        tspec(D_MODEL),
        pl.BlockSpec((1, 3, D_MODEL), lambda i, j: (i, 0, 0)),
        _const_spec((1, D_MODEL)),
        _const_spec((D_MODEL, _C_END)),
        _const_spec((1, MLA_Q_LORA)),
        _const_spec((1, MLA_KV_LORA)),
        _const_spec((MLA_Q_LORA, MLA_HEADS * MLA_QK_PAD)),
        _const_spec((MLA_Q_LORA, MLA_HEADS * MLA_QK_PAD)),
        _const_spec((MLA_KV_LORA, MLA_HEADS * MLA_QK_PAD)),
        _const_spec((MLA_KV_LORA, MLA_HEADS * MLA_V)),
    ]
    args = [x, mods3, g_pre, wts["w1"], wts["q_norm"], wts["kv_norm"], wts["w_uq"], wts["w_uq_sw"],
            wts["w_uk"], wts["w_uv"]]
    if latent:
        in_specs += [pl.BlockSpec((tm, LANES), lambda i, j: (j, 0))] * 2
        args += [rope[0], rope[1]]
    else:
        out_shape += [tok(MLA_KV_LORA, F32), tok(LANES, F32)]
        out_specs += [tspec(MLA_KV_LORA), tspec(LANES)]
    return pl.pallas_call(
        functools.partial(_inproj_kernel, latent=latent),
        out_shape=out_shape,
        grid=(b, n // tm),
        in_specs=in_specs,
        out_specs=out_specs,
        compiler_params=_params("arbitrary", "arbitrary"),
        name="inproj_lat" if latent else "inproj_ctx",
    )(*args)


def _cache_kv_kernel(ckv_ref, kr_ref, wuk_ref, wuv_ref, km_ref, vm_ref):
    ckv = ckv_ref[0, 0].astype(BF)
    kr = kr_ref[0, 0]
    kn = _dot(ckv, wuk_ref[0])
    vm_ref[0, 0] = _dot(ckv, wuv_ref[0]).astype(BF)
    for hd in range(MLA_HEADS):
        sl = slice(hd * MLA_QK_PAD, (hd + 1) * MLA_QK_PAD)
        km_ref[0, 0, :, sl] = (kn[:, sl] + kr).astype(BF)


def _cache_kv(ckv_c, kr_pad_c, w_uk, w_uv):
    _, b, p, _ = ckv_c.shape
    spec = lambda w: pl.BlockSpec((1, 1, p, w), lambda l, i: (l, i, 0, 0))
    wspec = lambda r, w: pl.BlockSpec((1, r, w), lambda l, i: (l, 0, 0))
    return pl.pallas_call(
        _cache_kv_kernel,
        out_shape=[jax.ShapeDtypeStruct((DEPTH, b, p, MLA_HEADS * MLA_QK_PAD), BF),
                   jax.ShapeDtypeStruct((DEPTH, b, p, MLA_HEADS * MLA_V), BF)],
        grid=(DEPTH, b),
        in_specs=[spec(MLA_KV_LORA), spec(LANES),
                  wspec(MLA_KV_LORA, MLA_HEADS * MLA_QK_PAD), wspec(MLA_KV_LORA, MLA_HEADS * MLA_V)],
        out_specs=[spec(MLA_HEADS * MLA_QK_PAD), spec(MLA_HEADS * MLA_V)],
        compiler_params=_params("arbitrary", "arbitrary"),
        name="cache_kv",
    )(ckv_c, kr_pad_c, w_uk, w_uv)


def _half_mask(width, j):
    lane = lax.broadcasted_iota(jnp.int32, (1, width), 1)
    return (lane // NA_HEAD_DIM) == j


def _softmax_pv(scores, values):
    m = None
    for s in scores:
        mi = jnp.max(s, axis=-1, keepdims=True)
        m = mi if m is None else jnp.maximum(m, mi)
    l = None
    o = None
    for s, v in zip(scores, values):
        p = jnp.exp(s - m)
        li = jnp.sum(p, axis=-1, keepdims=True)
        oi = _dot(p.astype(BF), v)
        l = li if l is None else l + li
        o = oi if o is None else o + oi
    return o * (1.0 / l)


def _ctx_attn_kernel(q_ref, k_ref, v_ref, qm_ref, km_ref, vm_ref, na_ref, mla_ref):
    zero = jnp.zeros((), BF)
    for hp in range(NA_HEADS // 2):
        sl = slice(hp * LANES, (hp + 1) * LANES)
        qblk = q_ref[0, :, sl]
        kblk = k_ref[0, :, sl].astype(BF)
        vblk = v_ref[0, :, sl].astype(BF)
        vmblk = vm_ref[0, :, sl]
        acc_na = None
        acc_mla = None
        for j in range(2):
            hm = _half_mask(LANES, j)
            s = _dot_nt(jnp.where(hm, qblk, zero), kblk)
            o = _softmax_pv([s], [jnp.where(hm, vblk, zero)])
            acc_na = o if acc_na is None else acc_na + o
            hd = 2 * hp + j
            hsl = slice(hd * MLA_QK_PAD, (hd + 1) * MLA_QK_PAD)
            s = _dot_nt(qm_ref[0, :, hsl], km_ref[0, :, hsl])
            o = _softmax_pv([s], [jnp.where(hm, vmblk, zero)])
            acc_mla = o if acc_mla is None else acc_mla + o
        na_ref[0, :, sl] = acc_na.astype(BF)
        mla_ref[0, :, sl] = acc_mla.astype(BF)


def _ctx_attn(q, k, v, qm, km, vm):
    b, s, _ = q.shape
    spec = lambda w: pl.BlockSpec((1, s, w), lambda i: (i, 0, 0))
    return pl.pallas_call(
        _ctx_attn_kernel,
        out_shape=[jax.ShapeDtypeStruct((b, s, BRANCH_W), BF)] * 2,
        grid=(b,),
        in_specs=[spec(512), spec(512), spec(512), spec(1024), spec(1024), spec(512)],
        out_specs=[spec(BRANCH_W), spec(BRANCH_W)],
        compiler_params=_params("arbitrary"),
        name="ctx_attn",
    )(q, k, v, qm, km, vm)


def _lat_mla_kernel(q_ref, k_ref, v_ref, kc_ref, vc_ref, o_ref):
    zero = jnp.zeros((), BF)
    for hp in range(MLA_HEADS // 2):
        sl = slice(hp * LANES, (hp + 1) * LANES)
        vblk = v_ref[0, :, sl]
        vcblk = vc_ref[0, :, sl]
        acc = None
        for j in range(2):
            hm = _half_mask(LANES, j)
            hd = 2 * hp + j
            hsl = slice(hd * MLA_QK_PAD, (hd + 1) * MLA_QK_PAD)
            q = q_ref[0, :, hsl]
            s = _dot_nt(q, k_ref[0, :, hsl])
            sc = _dot_nt(q, kc_ref[0, :, hsl])
            o = _softmax_pv([s, sc], [jnp.where(hm, vblk, zero), jnp.where(hm, vcblk, zero)])
            acc = o if acc is None else acc + o
        o_ref[0, :, sl] = acc.astype(BF)


def _lat_mla(qm, km, vm, kc, vc, tq):
    b, n, _ = qm.shape
    p = kc.shape[1]
    return pl.pallas_call(
        _lat_mla_kernel,
        out_shape=jax.ShapeDtypeStruct((b, n, BRANCH_W), BF),
        grid=(b, n // tq),
        in_specs=[
            pl.BlockSpec((1, tq, 1024), lambda i, j: (i, j, 0)),
            pl.BlockSpec((1, n, 1024), lambda i, j: (i, 0, 0)),
            pl.BlockSpec((1, n, 512), lambda i, j: (i, 0, 0)),
            pl.BlockSpec((1, p, 1024), lambda i, j: (i, 0, 0)),
            pl.BlockSpec((1, p, 512), lambda i, j: (i, 0, 0)),
        ],
        out_specs=pl.BlockSpec((1, tq, BRANCH_W), lambda i, j: (i, j, 0)),
        compiler_params=_params("arbitrary", "arbitrary"),
        name="lat_mla",
    )(qm, km, vm, kc, vc)


_RPB_ROWS = 2 * NA_WIN_H - 1
_RPB_COLS = 2 * NA_WIN_W - 1


def _na_bias_kernel(rpb_ref, o_ref, *, rows):
    base = (pl.program_id(0) * NA_HEADS + pl.program_id(1)) * (_RPB_ROWS * _RPB_COLS)
    shape = (GRID_W, 2 * GRID_W)
    qc = lax.broadcasted_iota(jnp.int32, shape, 0)
    lane = lax.broadcasted_iota(jnp.int32, shape, 1)
    kc = lane & (GRID_W - 1)
    upper = lane >= GRID_W
    ws = jnp.clip(qc - NA_WIN_W // 2, 0, GRID_W - NA_WIN_W)
    in_cols = (kc >= ws) & (kc < ws + NA_WIN_W)
    cidx = kc - qc + (NA_WIN_W - 1)
    col_is = [cidx == j for j in range(_RPB_COLS)]
    neg = jnp.full(shape, NEG_INF, F32)
    tiles = {}

    def pair_tile(ia, ib):
        if (ia, ib) not in tiles:
            if ia is None and ib is None:
                tiles[(ia, ib)] = neg
            else:
                acc = neg
                for j in range(_RPB_COLS):
                    sa = NEG_INF if ia is None else rpb_ref[base + ia * _RPB_COLS + j]
                    sb = NEG_INF if ib is None else rpb_ref[base + ib * _RPB_COLS + j]
                    acc = jnp.where(col_is[j], jnp.where(upper, sb, sa), acc)
                valid = in_cols
                if ia is None:
                    valid = valid & upper
                if ib is None:
                    valid = valid & jnp.logical_not(upper)
                tiles[(ia, ib)] = jnp.where(valid, acc, neg)
        return tiles[(ia, ib)]

    for kind, r0 in enumerate((0, 2 * NA_QROWS, rows - NA_QROWS)):
        ks = int(np.clip(r0 - NA_WIN_H // 2, 0, rows - NA_KROWS))
        for qr in range(NA_QROWS):
            r = r0 + qr
            rs = int(np.clip(r - NA_WIN_H // 2, 0, rows - NA_WIN_H))
            for m in range(NA_KROWS // 2):
                idx = [(ks + krl - r + NA_WIN_H - 1) if rs <= ks + krl < rs + NA_WIN_H else None
                       for krl in (2 * m, 2 * m + 1)]
                o_ref[0, kind, 0, qr * GRID_W:(qr + 1) * GRID_W, m * 2 * GRID_W:(m + 1) * 2 * GRID_W] = (
                    pair_tile(*idx))


def _na_bias_tables(rpb, rows):
    tq, nk = NA_QROWS * GRID_W, NA_KROWS * GRID_W
    return pl.pallas_call(
        functools.partial(_na_bias_kernel, rows=rows),
        out_shape=jax.ShapeDtypeStruct((DEPTH, 3, NA_HEADS, tq, nk), F32),
        grid=(DEPTH, NA_HEADS),
        in_specs=[pl.BlockSpec(memory_space=pltpu.SMEM)],
        out_specs=pl.BlockSpec((1, 3, 1, tq, nk), lambda l, h: (l, 0, h, 0, 0)),
        compiler_params=_params("arbitrary", "arbitrary"),
        name="na_bias",
    )(rpb.reshape(-1))


def _lat_na_kernel(q_ref, k_ref, v_ref, kc_ref, vc_ref, bias_ref, o_ref, *, rows):
    g = pl.program_id(1)
    ks = jnp.clip(NA_QROWS * g - NA_WIN_H // 2, 0, rows - NA_KROWS)
    k0 = pl.multiple_of(ks * GRID_W, GRID_W)
    nk = NA_KROWS * GRID_W
    zero = jnp.zeros((), BF)
    for hp in range(NA_HEADS // 2):
        sl = slice(hp * LANES, (hp + 1) * LANES)
        qblk = q_ref[0, :, sl]
        kblk = k_ref[0, pl.ds(k0, nk), sl]
        vblk = v_ref[0, pl.ds(k0, nk), sl]
        kcblk = kc_ref[0, :, sl]
        vcblk = vc_ref[0, :, sl]
        acc = None
        for j in range(2):
            hm = _half_mask(LANES, j)
            qh = jnp.where(hm, qblk, zero)
            s = _dot_nt(qh, kblk) + bias_ref[0, 2 * hp + j]
            sc = _dot_nt(qh, kcblk)
            o = _softmax_pv([s, sc], [jnp.where(hm, vblk, zero), jnp.where(hm, vcblk, zero)])
            acc = o if acc is None else acc + o
        o_ref[0, :, sl] = acc.astype(BF)


def _lat_na(q, k, v, kc, vc, bias, layer):
    b, n, _ = q.shape
    p = kc.shape[1]
    rows = n // GRID_W
    ngroups = rows // NA_QROWS
    tq = NA_QROWS * GRID_W
    nk = NA_KROWS * GRID_W

    def bias_map(i, g):
        kind = jnp.where(g == 0, 0, jnp.where(g == ngroups - 1, 2, 1))
        return (layer, kind, 0, 0, 0)

    return pl.pallas_call(
        functools.partial(_lat_na_kernel, rows=rows),
        out_shape=jax.ShapeDtypeStruct((b, n, BRANCH_W), BF),
        grid=(b, ngroups),
        in_specs=[
            pl.BlockSpec((1, tq, 512), lambda i, g: (i, g, 0)),
            pl.BlockSpec((1, n, 512), lambda i, g: (i, 0, 0)),
            pl.BlockSpec((1, n, 512), lambda i, g: (i, 0, 0)),
            pl.BlockSpec((1, p, 512), lambda i, g: (i, 0, 0)),
            pl.BlockSpec((1, p, 512), lambda i, g: (i, 0, 0)),
            pl.BlockSpec((None, 1, NA_HEADS, tq, nk), bias_map),
        ],
        out_specs=pl.BlockSpec((1, tq, BRANCH_W), lambda i, g: (i, g, 0)),
        compiler_params=_params("arbitrary", "arbitrary"),
        name="lat_na",
    )(q, k, v, kc, vc, bias)


def _merge_kernel(x_ref, mod_ref, nrm_ref, na_ref, mla_ref, pool_ref, prev_ref, next_ref,
                  wg_ref, wb_ref, wo_ref, pw_ref, ps_ref, o_ref, ext_ref, *, seq_len):
    tm = x_ref.shape[1]
    t = pl.program_id(1)
    nt = pl.num_programs(1)
    x = x_ref[0]
    shift, scale, gate = mod_ref[0, 0:1, :], mod_ref[0, 1:2, :], mod_ref[0, 2:3, :]
    u = (_rms(x, nrm_ref[0:1, :]) * (1.0 + scale) + shift).astype(BF)

    cur = pool_ref[0]
    ext_ref[0:POOL_HALO, :] = jnp.where(t > 0, prev_ref[0], 0.0)
    ext_ref[POOL_HALO:POOL_HALO + tm, :] = cur
    ext_ref[POOL_HALO + tm:POOL_HALO + tm + POOL_HALO, :] = jnp.where(t < nt - 1, next_ref[0], 0.0)
    pos = t * tm + lax.broadcasted_iota(jnp.int32, (tm, 1), 0)
    pooled = []
    for gi, w in enumerate(POOL_WINDOWS):
        sl = slice(gi * POOL_GROUP_DIM, (gi + 1) * POOL_GROUP_DIM)
        tot = None
        for j in range(-(w // 2), w // 2):
            piece = ext_ref[POOL_HALO + j:POOL_HALO + j + tm, sl]
            tot = piece if tot is None else tot + piece
        cnt = (jnp.minimum(pos + w // 2, seq_len) - jnp.maximum(pos - w // 2, 0)).astype(F32)
        diff = (tot / cnt - cur[:, sl]).astype(BF)
        pooled.append(_dot(diff, pw_ref[gi]))
    pool_o = (jnp.concatenate(pooled, axis=-1) * ps_ref[...]).astype(BF)

    branches = (na_ref[0], pool_o, mla_ref[0])
    z = None
    for kb in range(3):
        y = _dot(branches[kb], wb_ref[kb])
        gk = _sigmoid(_dot(u, wg_ref[:, kb * D_MODEL:(kb + 1) * D_MODEL]))
        z = gk * y if z is None else z + gk * y
    y = _dot(z.astype(BF), wo_ref[...])
    o_ref[0] = x + gate * _rms(y, nrm_ref[1:2, :])


def _merge(x, mods3, nrm2, na_o, mla_o, pool_in, wts, tm):
    b, n, _ = x.shape
    nh = tm // POOL_HALO
    last_h = n // POOL_HALO - 1
    tspec = lambda w: pl.BlockSpec((1, tm, w), lambda i, j: (i, j, 0))
    return pl.pallas_call(
        functools.partial(_merge_kernel, seq_len=n),
        out_shape=jax.ShapeDtypeStruct(x.shape, F32),
        grid=(b, n // tm),
        in_specs=[
            tspec(D_MODEL),
            pl.BlockSpec((1, 3, D_MODEL), lambda i, j: (i, 0, 0)),
            _const_spec((2, D_MODEL)),
            tspec(BRANCH_W), tspec(BRANCH_W), tspec(512),
            pl.BlockSpec((1, POOL_HALO, 512), lambda i, j: (i, jnp.maximum(j * nh - 1, 0), 0)),
            pl.BlockSpec((1, POOL_HALO, 512), lambda i, j: (i, jnp.minimum((j + 1) * nh, last_h), 0)),
            _const_spec((D_MODEL, 3 * D_MODEL)),
            _const_spec((3, BRANCH_W, D_MODEL)),
            _const_spec((D_MODEL, D_MODEL)),
            _const_spec((POOL_GROUPS, POOL_GROUP_DIM, POOL_GROUP_DIM)),
            _const_spec((1, 512)),
        ],
        out_specs=tspec(D_MODEL),
        scratch_shapes=[pltpu.VMEM((tm + 2 * POOL_HALO, 512), F32)],
        compiler_params=_params("arbitrary", "arbitrary"),
        name="merge",
    )(x, mods3, nrm2, na_o, mla_o, pool_in, pool_in, pool_in,
      wts["w_g"], wts["w_branch"], wts["w_out"], wts["pool_w"], wts["pool_scale"])


def _rope_tables(n_tok):
    half = MLA_ROPE // 2
    n_freq = half // 2
    inv = ROPE_BASE ** (-jnp.arange(n_freq, dtype=F32) / n_freq)
    t = jnp.arange(n_tok)
    ang_r = (t // GRID_W).astype(F32)[:, None] * inv
    ang_c = (t % GRID_W).astype(F32)[:, None] * inv
    ones = jnp.ones((n_tok, MLA_NOPE), F32)
    tail = jnp.zeros((n_tok, MLA_QK_PAD - MLA_NOPE - MLA_ROPE), F32)
    cos = jnp.concatenate([ones, jnp.cos(ang_r), jnp.cos(ang_r), jnp.cos(ang_c), jnp.cos(ang_c), tail + 1.0], axis=-1)
    sin = jnp.concatenate([0.0 * ones, -jnp.sin(ang_r), jnp.sin(ang_r), -jnp.sin(ang_c), jnp.sin(ang_c), tail], axis=-1)
    return cos, sin


def _rope_partner_perm():
    n_freq = MLA_ROPE // 4
    idx = np.arange(MLA_ROPE)
    return np.where((idx // n_freq) % 2 == 0, idx + n_freq, idx - n_freq)


def _pad_heads(w, parts):
    r, h, _ = w.shape
    cols = jnp.concatenate(parts, axis=-1)
    pad = MLA_QK_PAD - cols.shape[-1]
    return jnp.pad(cols, ((0, 0), (0, 0), (0, pad))).reshape(r, h * MLA_QK_PAD)


def _layer_weights(l, w_in, pool_w, pool_scale, q_norm, kv_norm, w_uq, w_ukv, w_branch, w_out):
    perm = _rope_partner_perm()
    bounds = np.cumsum([512, 512, 512, 512, MLA_Q_LORA, MLA_KV_LORA, MLA_ROPE])
    wi = w_in[l]
    kr_cols = wi[:, bounds[5]:bounds[6]]
    lane_pad = lambda c: jnp.pad(c, ((0, 0), (MLA_NOPE, MLA_QK_PAD - MLA_NOPE - MLA_ROPE)))
    w1 = jnp.concatenate([wi[:, :bounds[5]], lane_pad(kr_cols), lane_pad(kr_cols[:, perm])], axis=-1).astype(BF)
    uq = w_uq[l].reshape(MLA_Q_LORA, MLA_HEADS, MLA_NOPE + MLA_ROPE)
    ukv = w_ukv[l].reshape(MLA_KV_LORA, MLA_HEADS, MLA_NOPE + MLA_V)
    zeros_nope = jnp.zeros((MLA_Q_LORA, MLA_HEADS, MLA_NOPE), F32)
    return {
        "w1": w1,
        "w_g": wi[:, bounds[6]:].astype(BF),
        "q_norm": q_norm[l][None, :],
        "kv_norm": kv_norm[l][None, :],
        "w_uq": _pad_heads(uq, [uq]).astype(BF),
        "w_uq_sw": _pad_heads(uq, [zeros_nope, uq[..., MLA_NOPE:][..., perm]]).astype(BF),
        "w_uk": _pad_heads(ukv, [ukv[..., :MLA_NOPE]]).astype(BF),
        "w_uv": ukv[..., MLA_NOPE:].reshape(MLA_KV_LORA, MLA_HEADS * MLA_V).astype(BF),
        "w_branch": w_branch[l].astype(BF),
        "w_out": w_out[l].astype(BF),
        "pool_w": pool_w[l].astype(BF),
        "pool_scale": pool_scale[l][None, :],
    }


def kernel(x_prompt, x_sample, cache_na_k, cache_na_v, cache_mla_ckv, cache_mla_krope, c, c_ctx, w_ada, b_ada, norm_pre, norm_post, ffn_w_in, ffn_w_out, w_in, na_rpb, pool_w, pool_scale, mla_q_norm, mla_kv_norm, mla_w_uq, mla_w_ukv, w_branch, w_out):
    batch, seq, _ = x_prompt.shape
    dec_batch, dec_seq, _ = x_sample.shape
    past = cache_na_k.shape[2]
    rows = dec_seq // GRID_W

    n_rows = -(-(1 + dec_batch) // 8) * 8
    cvec = jnp.concatenate([c_ctx[None, :], c, jnp.zeros((n_rows - 1 - dec_batch, D_MODEL), F32)], axis=0)
    mods = _modulations(cvec, w_ada, b_ada)

    lw = [_layer_weights(l, w_in, pool_w, pool_scale, mla_q_norm, mla_kv_norm, mla_w_uq, mla_w_ukv,
                         w_branch, w_out) for l in range(DEPTH)]
    rope = _rope_tables(dec_seq)
    bias = _na_bias_tables(na_rpb, rows)
    ffn_in = ffn_w_in.astype(BF)
    ffn_out = ffn_w_out.astype(BF)

    kc_na = cache_na_k.reshape(dec_batch, DEPTH, past, 512).transpose(1, 0, 2, 3).astype(BF)
    vc_na = cache_na_v.reshape(dec_batch, DEPTH, past, 512).transpose(1, 0, 2, 3).astype(BF)
    ckv_c = cache_mla_ckv.transpose(1, 0, 2, 3)
    kr_c = jnp.pad(cache_mla_krope.transpose(1, 0, 2, 3),
                   ((0, 0), (0, 0), (0, 0), (MLA_NOPE, MLA_QK_PAD - MLA_NOPE - MLA_ROPE)))
    kc_mla, vc_mla = _cache_kv(ckv_c, kr_c, jnp.stack([w["w_uk"] for w in lw]), jnp.stack([w["w_uv"] for w in lw]))

    xp = x_prompt.reshape(1, batch * seq, D_MODEL)
    xs = x_sample
    states = [[], [], [], []]
    tm = 512
    for l in range(DEPTH):
        m_ctx, m_lat = mods[l, 0:1], mods[l, 1:1 + dec_batch]
        nrm = lambda j: jnp.stack([norm_pre[l, j], norm_post[l, j]])
        w = lw[l]
        xp = _ffn(xp, m_ctx[:, 0:3], nrm(0), ffn_in[l, 0], ffn_out[l, 0], tm)
        xs = _ffn(xs, m_lat[:, 0:3], nrm(0), ffn_in[l, 0], ffn_out[l, 0], tm)

        q, k, v, pool_in, qm, km, vm, ckv, kr = _inproj(
            xp, m_ctx[:, 3:6], norm_pre[l, 1][None, :], w, None, tm, latent=False)
        per_seq = lambda a: a.reshape(batch, seq, a.shape[-1])
        na_o, mla_o = _ctx_attn(*(per_seq(a) for a in (q, k, v, qm, km, vm)))
        xp = _merge(xp.reshape(batch, seq, D_MODEL), jnp.broadcast_to(m_ctx[:, 3:6], (batch, 3, D_MODEL)),
                    nrm(1), na_o, mla_o, per_seq(pool_in), w, seq).reshape(1, batch * seq, D_MODEL)
        states[0].append(k.reshape(batch, seq, NA_HEADS, NA_HEAD_DIM))
        states[1].append(v.reshape(batch, seq, NA_HEADS, NA_HEAD_DIM))
        states[2].append(ckv.reshape(batch, seq, MLA_KV_LORA))
        states[3].append(kr.reshape(batch, seq, LANES)[..., MLA_NOPE:MLA_NOPE + MLA_ROPE])

        q, k, v, pool_in, qm, km, vm = _inproj(xs, m_lat[:, 3:6], norm_pre[l, 1][None, :], w, rope, tm, latent=True)
        na_o = _lat_na(q, k, v, kc_na[l], vc_na[l], bias, l)
        mla_o = _lat_mla(qm, km, vm, kc_mla[l], vc_mla[l], 256)
        xs = _merge(xs, m_lat[:, 3:6], nrm(1), na_o, mla_o, pool_in, w, tm)

        xp = _ffn(xp, m_ctx[:, 6:9], nrm(2), ffn_in[l, 1], ffn_out[l, 1], tm)
        xs = _ffn(xs, m_lat[:, 6:9], nrm(2), ffn_in[l, 1], ffn_out[l, 1], tm)

    return (xp.reshape(batch, seq, D_MODEL), xs) + tuple(jnp.stack(s, axis=1) for s in states)
```

```python
import functools

import numpy as np
import jax
import jax.numpy as jnp
from jax import lax
from jax.experimental import pallas as pl
from jax.experimental.pallas import tpu as pltpu

D_MODEL = 1024
DEPTH = 2
GRID_W = 64
N_MOD = 9
FFN_DIM = 2816
BRANCH_W = 512
NA_HEADS = 8
NA_HEAD_DIM = 64
NA_WIN_H = 8
NA_WIN_W = 16
POOL_GROUPS = 4
POOL_GROUP_DIM = 128
POOL_WINDOWS = (2, 4, 8, 16)
POOL_HALO = 8
MLA_HEADS = 8
MLA_Q_LORA = 256
MLA_KV_LORA = 256
MLA_NOPE = 64
MLA_ROPE = 32
MLA_V = 64
ROPE_BASE = 10000.0
NORM_EPS = 1e-6
NEG_INF = -1e30

LANES = 128
MLA_QK_PAD = LANES
VMEM_LIMIT_BYTES = 56 * 1024 * 1024

NA_QROWS = 4
NA_KROWS = 12
NA_KCHUNK = 512

BF = jnp.bfloat16
F32 = jnp.float32

_NT = (((1,), (1,)), ((), ()))


def _dot(a, b):
    return jnp.dot(a, b, preferred_element_type=F32)


def _dot_nt(a, b):
    return lax.dot_general(a, b, _NT, preferred_element_type=F32)


def _sigmoid(x):
    return 1.0 / (1.0 + jnp.exp(-x))


def _silu(x):
    return x * _sigmoid(x)


def _rms(x, g):
    ms = jnp.mean(x * x, axis=-1, keepdims=True)
    return (x * lax.rsqrt(ms + NORM_EPS)) * g


def _const_spec(shape):
    nd = len(shape)
    return pl.BlockSpec(shape, lambda *_: (0,) * nd, pipeline_mode=pl.Buffered(1))


def _params(*sem):
    return pltpu.CompilerParams(dimension_semantics=sem, vmem_limit_bytes=VMEM_LIMIT_BYTES)


def _mods_kernel(c_ref, w_ref, b_ref, o_ref):
    a = _silu(c_ref[...]).astype(BF)
    o_ref[0] = _dot(a, w_ref[0].astype(BF)) + b_ref[0]


def _modulations(cvec, w_ada, b_ada):
    r = cvec.shape[0]
    ncol = N_MOD * D_MODEL
    tn = 1024
    out = pl.pallas_call(
        _mods_kernel,
        out_shape=jax.ShapeDtypeStruct((DEPTH, r, ncol), F32),
        grid=(DEPTH, ncol // tn),
        in_specs=[
            pl.BlockSpec((r, D_MODEL), lambda l, j: (0, 0)),
            pl.BlockSpec((1, D_MODEL, tn), lambda l, j: (l, 0, j)),
            pl.BlockSpec((1, 1, tn), lambda l, j: (l, 0, j)),
        ],
        out_specs=pl.BlockSpec((1, r, tn), lambda l, j: (l, 0, j)),
        compiler_params=_params("arbitrary", "arbitrary"),
        name="mods",
    )(cvec, w_ada, b_ada.reshape(DEPTH, 1, ncol))
    return out.reshape(DEPTH, r, N_MOD, D_MODEL)


_FFN_CHUNKS = ((0, 512), (512, 512), (1024, 512), (1536, 512), (2048, 512), (2560, 256))


def _ffn_kernel(x_ref, mod_ref, nrm_ref, win_ref, wout_ref, o_ref):
    x = x_ref[0]
    shift, scale, gate = mod_ref[0, 0:1, :], mod_ref[0, 1:2, :], mod_ref[0, 2:3, :]
    u = (_rms(x, nrm_ref[0:1, :]) * (1.0 + scale) + shift).astype(BF)
    acc = None
    for off, ck in _FFN_CHUNKS:
        g = _dot(u, win_ref[:, off:off + ck])
        up = _dot(u, win_ref[:, FFN_DIM + off:FFN_DIM + off + ck])
        d = _dot((_silu(g) * up).astype(BF), wout_ref[off:off + ck, :])
        acc = d if acc is None else acc + d
    o_ref[0] = x + 0.5 * (gate * _rms(acc, nrm_ref[1:2, :]))


def _ffn(x, mods3, nrm2, w_in, w_out, tm):
    b, n, _ = x.shape
    return pl.pallas_call(
        _ffn_kernel,
        out_shape=jax.ShapeDtypeStruct(x.shape, F32),
        grid=(b, n // tm),
        in_specs=[
            pl.BlockSpec((1, tm, D_MODEL), lambda i, j: (i, j, 0)),
            pl.BlockSpec((1, 3, D_MODEL), lambda i, j: (i, 0, 0)),
            _const_spec((2, D_MODEL)),
            _const_spec((D_MODEL, 2 * FFN_DIM)),
            _const_spec((FFN_DIM, D_MODEL)),
        ],
        out_specs=pl.BlockSpec((1, tm, D_MODEL), lambda i, j: (i, j, 0)),
        compiler_params=_params("arbitrary", "arbitrary"),
        name="ffn",
    )(x, mods3, nrm2, w_in, w_out)


_C_Q, _C_K, _C_V, _C_POOL, _C_CQ, _C_CKV, _C_KR, _C_KRS, _C_END = (
    0, 512, 1024, 1536, 2048, 2304, 2560, 2688, 2816)


def _inproj_kernel(*refs, latent):
    if latent:
        (x_ref, mod_ref, g_ref, w1_ref, qn_ref, kvn_ref, wuq_ref, wuqs_ref, wuk_ref, wuv_ref,
         cos_ref, sin_ref,
         q_ref, k_ref, v_ref, pool_ref, qm_ref, km_ref, vm_ref) = refs
    else:
        (x_ref, mod_ref, g_ref, w1_ref, qn_ref, kvn_ref, wuq_ref, wuqs_ref, wuk_ref, wuv_ref,
         q_ref, k_ref, v_ref, pool_ref, qm_ref, km_ref, vm_ref, ckv_ref, kr_ref) = refs
    x = x_ref[0]
    shift, scale = mod_ref[0, 0:1, :], mod_ref[0, 1:2, :]
    u = (_rms(x, g_ref[...]) * (1.0 + scale) + shift).astype(BF)
    h = _dot(u, w1_ref[...])
    q_ref[0] = (h[:, _C_Q:_C_K] * (NA_HEAD_DIM ** -0.5)).astype(q_ref.dtype)
    k_ref[0] = h[:, _C_K:_C_V].astype(k_ref.dtype)
    v_ref[0] = h[:, _C_V:_C_POOL].astype(v_ref.dtype)
    pool_ref[0] = h[:, _C_POOL:_C_CQ]

    qn = _rms(h[:, _C_CQ:_C_CKV], qn_ref[...]).astype(BF)
    ckv = _rms(h[:, _C_CKV:_C_KR], kvn_ref[...])
    ckv_b = ckv.astype(BF)
    kr = h[:, _C_KR:_C_KRS]
    qa = _dot(qn, wuq_ref[...])
    if latent:
        cos, sin = cos_ref[...], sin_ref[...]
        qb = _dot(qn, wuqs_ref[...])
        kr = kr * cos + h[:, _C_KRS:_C_END] * sin
    else:
        ckv_ref[0] = ckv
        kr_ref[0] = kr
    kn = _dot(ckv_b, wuk_ref[...])
    vm_ref[0] = _dot(ckv_b, wuv_ref[...]).astype(BF)
    qscale = (MLA_NOPE + MLA_ROPE) ** -0.5
    for hd in range(MLA_HEADS):
        sl = slice(hd * MLA_QK_PAD, (hd + 1) * MLA_QK_PAD)
        qh = qa[:, sl]
        if latent:
            qh = qh * cos + qb[:, sl] * sin
        qm_ref[0, :, sl] = (qh * qscale).astype(BF)
        km_ref[0, :, sl] = (kn[:, sl] + kr).astype(BF)


def _inproj(x, mods3, g_pre, wts, rope, tm, latent):
    b, n, _ = x.shape
    tok = lambda w, dt: jax.ShapeDtypeStruct((b, n, w), dt)
    tspec = lambda w: pl.BlockSpec((1, tm, w), lambda i, j: (i, j, 0))
    kv_dt = BF if latent else F32
    out_shape = [tok(512, BF), tok(512, kv_dt), tok(512, kv_dt), tok(512, F32),
                 tok(1024, BF), tok(1024, BF), tok(512, BF)]
    out_specs = [tspec(512), tspec(512), tspec(512), tspec(512), tspec(1024), tspec(1024), tspec(512)]
    in_specs = [
        tspec(D_MODEL),
        pl.BlockSpec((1, 3, D_MODEL), lambda i, j: (i, 0, 0)),
        _const_spec((1, D_MODEL)),
        _const_spec((D_MODEL, _C_END)),
        _const_spec((1, MLA_Q_LORA)),
        _const_spec((1, MLA_KV_LORA)),
        _const_spec((MLA_Q_LORA, MLA_HEADS * MLA_QK_PAD)),
        _const_spec((MLA_Q_LORA, MLA_HEADS * MLA_QK_PAD)),
        _const_spec((MLA_KV_LORA, MLA_HEADS * MLA_QK_PAD)),
        _const_spec((MLA_KV_LORA, MLA_HEADS * MLA_V)),
    ]
    args = [x, mods3, g_pre, wts["w1"], wts["q_norm"], wts["kv_norm"], wts["w_uq"], wts["w_uq_sw"],
            wts["w_uk"], wts["w_uv"]]
    if latent:
        in_specs += [pl.BlockSpec((tm, LANES), lambda i, j: (j, 0))] * 2
        args += [rope[0], rope[1]]
    else:
        out_shape += [tok(MLA_KV_LORA, F32), tok(LANES, F32)]
        out_specs += [tspec(MLA_KV_LORA), tspec(LANES)]
    return pl.pallas_call(
        functools.partial(_inproj_kernel, latent=latent),
        out_shape=out_shape,
        grid=(b, n // tm),
        in_specs=in_specs,
        out_specs=out_specs,
        compiler_params=_params("arbitrary", "arbitrary"),
        name="inproj_lat" if latent else "inproj_ctx",
    )(*args)


def _cache_kv_kernel(ckv_ref, kr_ref, wuk_ref, wuv_ref, km_ref, vm_ref):
    ckv = ckv_ref[0, 0].astype(BF)
    kr = kr_ref[0, 0]
    kn = _dot(ckv, wuk_ref[0])
    vm_ref[0, 0] = _dot(ckv, wuv_ref[0]).astype(BF)
    for hd in range(MLA_HEADS):
        sl = slice(hd * MLA_QK_PAD, (hd + 1) * MLA_QK_PAD)
        km_ref[0, 0, :, sl] = (kn[:, sl] + kr).astype(BF)


def _cache_kv(ckv_c, kr_pad_c, w_uk, w_uv):
    _, b, p, _ = ckv_c.shape
    spec = lambda w: pl.BlockSpec((1, 1, p, w), lambda l, i: (l, i, 0, 0))
    wspec = lambda r, w: pl.BlockSpec((1, r, w), lambda l, i: (l, 0, 0))
    return pl.pallas_call(
        _cache_kv_kernel,
        out_shape=[jax.ShapeDtypeStruct((DEPTH, b, p, MLA_HEADS * MLA_QK_PAD), BF),
                   jax.ShapeDtypeStruct((DEPTH, b, p, MLA_HEADS * MLA_V), BF)],
        grid=(DEPTH, b),
        in_specs=[spec(MLA_KV_LORA), spec(LANES),
                  wspec(MLA_KV_LORA, MLA_HEADS * MLA_QK_PAD), wspec(MLA_KV_LORA, MLA_HEADS * MLA_V)],
        out_specs=[spec(MLA_HEADS * MLA_QK_PAD), spec(MLA_HEADS * MLA_V)],
        compiler_params=_params("arbitrary", "arbitrary"),
        name="cache_kv",
    )(ckv_c, kr_pad_c, w_uk, w_uv)


def _half_mask(width, j):
    lane = lax.broadcasted_iota(jnp.int32, (1, width), 1)
    return (lane // NA_HEAD_DIM) == j


def _softmax_pv(scores, values):
    m = None
    for s in scores:
        mi = jnp.max(s, axis=-1, keepdims=True)
        m = mi if m is None else jnp.maximum(m, mi)
    l = None
    o = None
    for s, v in zip(scores, values):
        p = jnp.exp(s - m)
        li = jnp.sum(p, axis=-1, keepdims=True)
        oi = _dot(p.astype(BF), v)
        l = li if l is None else l + li
        o = oi if o is None else o + oi
    return o * (1.0 / l)


def _fold_lanes(x, op):
    out = x[:, 0:LANES]
    for t in range(1, x.shape[1] // LANES):
        out = op(out, x[:, t * LANES:(t + 1) * LANES])
    return out


def _interleaved_heads(n_heads, n_chunks, score_chunk, prob_chunk, weighted_values, o_ref):
    def combine(acc, x, op):
        return x if acc is None else op(acc, x)

    mrun = None
    for ci in range(n_chunks):
        mrun = combine(mrun, score_chunk(0, ci), jnp.maximum)
    m_next = jnp.max(mrun, axis=-1, keepdims=True)
    acc = None
    for hd in range(n_heads):
        m, mrun, lrun = m_next, None, None
        for ci in range(n_chunks):
            if hd + 1 < n_heads:
                mrun = combine(mrun, score_chunk(hd + 1, ci), jnp.maximum)
            lrun = combine(lrun, prob_chunk(hd, ci, m), jnp.add)
        if hd + 1 < n_heads:
            m_next = jnp.max(mrun, axis=-1, keepdims=True)
        o = weighted_values(hd) * (1.0 / jnp.sum(lrun, axis=-1, keepdims=True))
        acc = o if hd % 2 == 0 else acc + o
        if hd % 2 == 1:
            sl = slice((hd // 2) * LANES, (hd // 2 + 1) * LANES)
            o_ref[0, :, sl] = acc.astype(BF)


def _ctx_attn_kernel(q_ref, k_ref, v_ref, qm_ref, km_ref, vm_ref, na_ref, mla_ref):
    zero = jnp.zeros((), BF)
    for hp in range(NA_HEADS // 2):
        sl = slice(hp * LANES, (hp + 1) * LANES)
        qblk = q_ref[0, :, sl]
        kblk = k_ref[0, :, sl].astype(BF)
        vblk = v_ref[0, :, sl].astype(BF)
        vmblk = vm_ref[0, :, sl]
        acc_na = None
        acc_mla = None
        for j in range(2):
            hm = _half_mask(LANES, j)
            s = _dot_nt(jnp.where(hm, qblk, zero), kblk)
            o = _softmax_pv([s], [jnp.where(hm, vblk, zero)])
            acc_na = o if acc_na is None else acc_na + o
            hd = 2 * hp + j
            hsl = slice(hd * MLA_QK_PAD, (hd + 1) * MLA_QK_PAD)
            s = _dot_nt(qm_ref[0, :, hsl], km_ref[0, :, hsl])
            o = _softmax_pv([s], [jnp.where(hm, vmblk, zero)])
            acc_mla = o if acc_mla is None else acc_mla + o
        na_ref[0, :, sl] = acc_na.astype(BF)
        mla_ref[0, :, sl] = acc_mla.astype(BF)


def _ctx_attn(q, k, v, qm, km, vm):
    b, s, _ = q.shape
    spec = lambda w: pl.BlockSpec((1, s, w), lambda i: (i, 0, 0))
    return pl.pallas_call(
        _ctx_attn_kernel,
        out_shape=[jax.ShapeDtypeStruct((b, s, BRANCH_W), BF)] * 2,
        grid=(b,),
        in_specs=[spec(512), spec(512), spec(512), spec(1024), spec(1024), spec(512)],
        out_specs=[spec(BRANCH_W), spec(BRANCH_W)],
        compiler_params=_params("arbitrary"),
        name="ctx_attn",
    )(q, k, v, qm, km, vm)


MLA_KCHUNK = 512


def _lat_mla_kernel(q_ref, k_ref, v_ref, kc_ref, vc_ref, o_ref, s_ref, p_ref):
    n, pc = k_ref.shape[1], kc_ref.shape[1]
    chunks = [(k_ref, off, MLA_KCHUNK, off) for off in range(0, n, MLA_KCHUNK)] + [(kc_ref, 0, pc, n)]
    zero = jnp.zeros((), BF)

    def score_chunk(hd, ci):
        kref, off, w, col = chunks[ci]
        hsl = slice(hd * MLA_QK_PAD, (hd + 1) * MLA_QK_PAD)
        s = _dot_nt(q_ref[0, :, hsl], kref[0, off:off + w, hsl])
        s_ref[hd % 2, :, col:col + w] = s
        return _fold_lanes(s, jnp.maximum)

    def prob_chunk(hd, ci, m):
        _, _, w, col = chunks[ci]
        p = jnp.exp(s_ref[hd % 2, :, col:col + w] - m)
        p_ref[hd % 2, :, col:col + w] = p.astype(BF)
        return _fold_lanes(p, jnp.add)

    def weighted_values(hd):
        sl = slice((hd // 2) * LANES, (hd // 2 + 1) * LANES)
        hm = _half_mask(LANES, hd % 2)
        return (_dot(p_ref[hd % 2, :, 0:n], jnp.where(hm, v_ref[0, :, sl], zero))
                + _dot(p_ref[hd % 2, :, n:n + pc], jnp.where(hm, vc_ref[0, :, sl], zero)))

    _interleaved_heads(MLA_HEADS, len(chunks), score_chunk, prob_chunk, weighted_values, o_ref)


def _lat_mla(qm, km, vm, kc, vc, tq):
    b, n, _ = qm.shape
    p = kc.shape[1]
    return pl.pallas_call(
        _lat_mla_kernel,
        scratch_shapes=[pltpu.VMEM((2, tq, n + p), F32), pltpu.VMEM((2, tq, n + p), BF)],
        out_shape=jax.ShapeDtypeStruct((b, n, BRANCH_W), BF),
        grid=(b, n // tq),
        in_specs=[
            pl.BlockSpec((1, tq, 1024), lambda i, j: (i, j, 0)),
            pl.BlockSpec((1, n, 1024), lambda i, j: (i, 0, 0)),
            pl.BlockSpec((1, n, 512), lambda i, j: (i, 0, 0)),
            pl.BlockSpec((1, p, 1024), lambda i, j: (i, 0, 0)),
            pl.BlockSpec((1, p, 512), lambda i, j: (i, 0, 0)),
        ],
        out_specs=pl.BlockSpec((1, tq, BRANCH_W), lambda i, j: (i, j, 0)),
        compiler_params=_params("arbitrary", "arbitrary"),
        name="lat_mla",
    )(qm, km, vm, kc, vc)


_RPB_ROWS = 2 * NA_WIN_H - 1
_RPB_COLS = 2 * NA_WIN_W - 1


def _na_bias_kernel(rpb_ref, o_ref, *, rows):
    base = (pl.program_id(0) * NA_HEADS + pl.program_id(1)) * (_RPB_ROWS * _RPB_COLS)
    shape = (GRID_W, 2 * GRID_W)
    qc = lax.broadcasted_iota(jnp.int32, shape, 0)
    lane = lax.broadcasted_iota(jnp.int32, shape, 1)
    kc = lane & (GRID_W - 1)
    upper = lane >= GRID_W
    ws = jnp.clip(qc - NA_WIN_W // 2, 0, GRID_W - NA_WIN_W)
    in_cols = (kc >= ws) & (kc < ws + NA_WIN_W)
    cidx = kc - qc + (NA_WIN_W - 1)
    col_is = [cidx == j for j in range(_RPB_COLS)]
    neg = jnp.full(shape, NEG_INF, F32)
    tiles = {}

    def pair_tile(ia, ib):
        if (ia, ib) not in tiles:
            if ia is None and ib is None:
                tiles[(ia, ib)] = neg
            else:
                acc = neg
                for j in range(_RPB_COLS):
                    sa = NEG_INF if ia is None else rpb_ref[base + ia * _RPB_COLS + j]
                    sb = NEG_INF if ib is None else rpb_ref[base + ib * _RPB_COLS + j]
                    acc = jnp.where(col_is[j], jnp.where(upper, sb, sa), acc)
                valid = in_cols
                if ia is None:
                    valid = valid & upper
                if ib is None:
                    valid = valid & jnp.logical_not(upper)
                tiles[(ia, ib)] = jnp.where(valid, acc, neg)
        return tiles[(ia, ib)]

    for kind, r0 in enumerate((0, 2 * NA_QROWS, rows - NA_QROWS)):
        ks = int(np.clip(r0 - NA_WIN_H // 2, 0, rows - NA_KROWS))
        for qr in range(NA_QROWS):
            r = r0 + qr
            rs = int(np.clip(r - NA_WIN_H // 2, 0, rows - NA_WIN_H))
            for m in range(NA_KROWS // 2):
                idx = [(ks + krl - r + NA_WIN_H - 1) if rs <= ks + krl < rs + NA_WIN_H else None
                       for krl in (2 * m, 2 * m + 1)]
                o_ref[0, kind, 0, qr * GRID_W:(qr + 1) * GRID_W, m * 2 * GRID_W:(m + 1) * 2 * GRID_W] = (
                    pair_tile(*idx))


def _na_bias_tables(rpb, rows):
    tq, nk = NA_QROWS * GRID_W, NA_KROWS * GRID_W
    return pl.pallas_call(
        functools.partial(_na_bias_kernel, rows=rows),
        out_shape=jax.ShapeDtypeStruct((DEPTH, 3, NA_HEADS, tq, nk), F32),
        grid=(DEPTH, NA_HEADS),
        in_specs=[pl.BlockSpec(memory_space=pltpu.SMEM)],
        out_specs=pl.BlockSpec((1, 3, 1, tq, nk), lambda l, h: (l, 0, h, 0, 0)),
        compiler_params=_params("arbitrary", "arbitrary"),
        name="na_bias",
    )(rpb.reshape(-1))


def _lat_na_kernel(q_ref, k_ref, v_ref, kc_ref, vc_ref, bias_ref, o_ref, s_ref, p_ref, *, rows):
    g = pl.program_id(1)
    ks = jnp.clip(NA_QROWS * g - NA_WIN_H // 2, 0, rows - NA_KROWS)
    k0 = pl.multiple_of(ks * GRID_W, GRID_W)
    nk = NA_KROWS * GRID_W
    pc = kc_ref.shape[1]
    zero = jnp.zeros((), BF)
    local_chunks = [(off, min(NA_KCHUNK, nk - off)) for off in range(0, nk, NA_KCHUNK)]
    n_chunks = len(local_chunks) + 1

    def pair(hd):
        return slice((hd // 2) * LANES, (hd // 2 + 1) * LANES)

    def score_chunk(hd, ci):
        qh = jnp.where(_half_mask(LANES, hd % 2), q_ref[0, :, pair(hd)], zero)
        if ci < len(local_chunks):
            off, w = local_chunks[ci]
            kblk = k_ref[0, pl.ds(pl.multiple_of(k0 + off, GRID_W), w), pair(hd)]
            s = _dot_nt(qh, kblk) + bias_ref[0, hd, :, off:off + w]
        else:
            off, w = nk, pc
            s = _dot_nt(qh, kc_ref[0, :, pair(hd)])
        s_ref[hd % 2, :, off:off + w] = s
        return _fold_lanes(s, jnp.maximum)

    def prob_chunk(hd, ci, m):
        off, w = local_chunks[ci] if ci < len(local_chunks) else (nk, pc)
        p = jnp.exp(s_ref[hd % 2, :, off:off + w] - m)
        p_ref[hd % 2, :, off:off + w] = p.astype(BF)
        return _fold_lanes(p, jnp.add)

    def weighted_values(hd):
        hm = _half_mask(LANES, hd % 2)
        return (_dot(p_ref[hd % 2, :, 0:nk], jnp.where(hm, v_ref[0, pl.ds(k0, nk), pair(hd)], zero))
                + _dot(p_ref[hd % 2, :, nk:nk + pc], jnp.where(hm, vc_ref[0, :, pair(hd)], zero)))

    _interleaved_heads(NA_HEADS, n_chunks, score_chunk, prob_chunk, weighted_values, o_ref)


def _lat_na(q, k, v, kc, vc, bias, layer):
    b, n, _ = q.shape
    p = kc.shape[1]
    rows = n // GRID_W
    ngroups = rows // NA_QROWS
    tq = NA_QROWS * GRID_W
    nk = NA_KROWS * GRID_W

    def bias_map(i, g):
        kind = jnp.where(g == 0, 0, jnp.where(g == ngroups - 1, 2, 1))
        return (layer, kind, 0, 0, 0)

    return pl.pallas_call(
        functools.partial(_lat_na_kernel, rows=rows),
        out_shape=jax.ShapeDtypeStruct((b, n, BRANCH_W), BF),
        scratch_shapes=[pltpu.VMEM((2, tq, nk + p), F32), pltpu.VMEM((2, tq, nk + p), BF)],
        grid=(b, ngroups),
        in_specs=[
            pl.BlockSpec((1, tq, 512), lambda i, g: (i, g, 0)),
            pl.BlockSpec((1, n, 512), lambda i, g: (i, 0, 0)),
            pl.BlockSpec((1, n, 512), lambda i, g: (i, 0, 0)),
            pl.BlockSpec((1, p, 512), lambda i, g: (i, 0, 0)),
            pl.BlockSpec((1, p, 512), lambda i, g: (i, 0, 0)),
            pl.BlockSpec((None, 1, NA_HEADS, tq, nk), bias_map),
        ],
        out_specs=pl.BlockSpec((1, tq, BRANCH_W), lambda i, g: (i, g, 0)),
        compiler_params=_params("arbitrary", "arbitrary"),
        name="lat_na",
    )(q, k, v, kc, vc, bias)


def _merge_kernel(x_ref, mod_ref, nrm_ref, na_ref, mla_ref, pool_ref, prev_ref, next_ref,
                  wg_ref, wb_ref, wo_ref, pw_ref, ps_ref, o_ref, ext_ref, *, seq_len):
    tm = x_ref.shape[1]
    t = pl.program_id(1)
    nt = pl.num_programs(1)
    y_na = _dot(na_ref[0], wb_ref[0])
    y_mla = _dot(mla_ref[0], wb_ref[2])
    x = x_ref[0]
    shift, scale, gate = mod_ref[0, 0:1, :], mod_ref[0, 1:2, :], mod_ref[0, 2:3, :]
    u = (_rms(x, nrm_ref[0:1, :]) * (1.0 + scale) + shift).astype(BF)

    cur = pool_ref[0]
    ext_ref[0:POOL_HALO, :] = jnp.where(t > 0, prev_ref[0], 0.0)
    ext_ref[POOL_HALO:POOL_HALO + tm, :] = cur
    ext_ref[POOL_HALO + tm:POOL_HALO + tm + POOL_HALO, :] = jnp.where(t < nt - 1, next_ref[0], 0.0)
    pos = t * tm + lax.broadcasted_iota(jnp.int32, (tm, 1), 0)
    pooled = []
    for gi, w in enumerate(POOL_WINDOWS):
        sl = slice(gi * POOL_GROUP_DIM, (gi + 1) * POOL_GROUP_DIM)
        tot = None
        for j in range(-(w // 2), w // 2):
            piece = ext_ref[POOL_HALO + j:POOL_HALO + j + tm, sl]
            tot = piece if tot is None else tot + piece
        cnt = (jnp.minimum(pos + w // 2, seq_len) - jnp.maximum(pos - w // 2, 0)).astype(F32)
        diff = (tot / cnt - cur[:, sl]).astype(BF)
        pooled.append(_dot(diff, pw_ref[gi]))
    pool_o = (jnp.concatenate(pooled, axis=-1) * ps_ref[...]).astype(BF)

    ys = (y_na, _dot(pool_o, wb_ref[1]), y_mla)
    z = None
    for kb in range(3):
        gk = _sigmoid(_dot(u, wg_ref[:, kb * D_MODEL:(kb + 1) * D_MODEL]))
        z = gk * ys[kb] if z is None else z + gk * ys[kb]
    y = _dot(z.astype(BF), wo_ref[...])
    o_ref[0] = x + gate * _rms(y, nrm_ref[1:2, :])


def _merge(x, mods3, nrm2, na_o, mla_o, pool_in, wts, tm):
    b, n, _ = x.shape
    nh = tm // POOL_HALO
    last_h = n // POOL_HALO - 1
    tspec = lambda w: pl.BlockSpec((1, tm, w), lambda i, j: (i, j, 0))
    return pl.pallas_call(
        functools.partial(_merge_kernel, seq_len=n),
        out_shape=jax.ShapeDtypeStruct(x.shape, F32),
        grid=(b, n // tm),
        in_specs=[
            tspec(D_MODEL),
            pl.BlockSpec((1, 3, D_MODEL), lambda i, j: (i, 0, 0)),
            _const_spec((2, D_MODEL)),
            tspec(BRANCH_W), tspec(BRANCH_W), tspec(512),
            pl.BlockSpec((1, POOL_HALO, 512), lambda i, j: (i, jnp.maximum(j * nh - 1, 0), 0)),
            pl.BlockSpec((1, POOL_HALO, 512), lambda i, j: (i, jnp.minimum((j + 1) * nh, last_h), 0)),
            _const_spec((D_MODEL, 3 * D_MODEL)),
            _const_spec((3, BRANCH_W, D_MODEL)),
            _const_spec((D_MODEL, D_MODEL)),
            _const_spec((POOL_GROUPS, POOL_GROUP_DIM, POOL_GROUP_DIM)),
            _const_spec((1, 512)),
        ],
        out_specs=tspec(D_MODEL),
        scratch_shapes=[pltpu.VMEM((tm + 2 * POOL_HALO, 512), F32)],
        compiler_params=_params("arbitrary", "arbitrary"),
        name="merge",
    )(x, mods3, nrm2, na_o, mla_o, pool_in, pool_in, pool_in,
      wts["w_g"], wts["w_branch"], wts["w_out"], wts["pool_w"], wts["pool_scale"])


def _rope_tables(n_tok):
    half = MLA_ROPE // 2
    n_freq = half // 2
    inv = ROPE_BASE ** (-jnp.arange(n_freq, dtype=F32) / n_freq)
    t = jnp.arange(n_tok)
    ang_r = (t // GRID_W).astype(F32)[:, None] * inv
    ang_c = (t % GRID_W).astype(F32)[:, None] * inv
    ones = jnp.ones((n_tok, MLA_NOPE), F32)
    tail = jnp.zeros((n_tok, MLA_QK_PAD - MLA_NOPE - MLA_ROPE), F32)
    cos = jnp.concatenate([ones, jnp.cos(ang_r), jnp.cos(ang_r), jnp.cos(ang_c), jnp.cos(ang_c), tail + 1.0], axis=-1)
    sin = jnp.concatenate([0.0 * ones, -jnp.sin(ang_r), jnp.sin(ang_r), -jnp.sin(ang_c), jnp.sin(ang_c), tail], axis=-1)
    return cos, sin


def _rope_partner_perm():
    n_freq = MLA_ROPE // 4
    idx = np.arange(MLA_ROPE)
    return np.where((idx // n_freq) % 2 == 0, idx + n_freq, idx - n_freq)


def _pad_heads(w, parts):
    r, h, _ = w.shape
    cols = jnp.concatenate(parts, axis=-1)
    pad = MLA_QK_PAD - cols.shape[-1]
    return jnp.pad(cols, ((0, 0), (0, 0), (0, pad))).reshape(r, h * MLA_QK_PAD)


def _layer_weights(l, w_in, pool_w, pool_scale, q_norm, kv_norm, w_uq, w_ukv, w_branch, w_out):
    perm = _rope_partner_perm()
    bounds = np.cumsum([512, 512, 512, 512, MLA_Q_LORA, MLA_KV_LORA, MLA_ROPE])
    wi = w_in[l]
    kr_cols = wi[:, bounds[5]:bounds[6]]
    lane_pad = lambda c: jnp.pad(c, ((0, 0), (MLA_NOPE, MLA_QK_PAD - MLA_NOPE - MLA_ROPE)))
    w1 = jnp.concatenate([wi[:, :bounds[5]], lane_pad(kr_cols), lane_pad(kr_cols[:, perm])], axis=-1).astype(BF)
    uq = w_uq[l].reshape(MLA_Q_LORA, MLA_HEADS, MLA_NOPE + MLA_ROPE)
    ukv = w_ukv[l].reshape(MLA_KV_LORA, MLA_HEADS, MLA_NOPE + MLA_V)
    zeros_nope = jnp.zeros((MLA_Q_LORA, MLA_HEADS, MLA_NOPE), F32)
    return {
        "w1": w1,
        "w_g": wi[:, bounds[6]:].astype(BF),
        "q_norm": q_norm[l][None, :],
        "kv_norm": kv_norm[l][None, :],
        "w_uq": _pad_heads(uq, [uq]).astype(BF),
        "w_uq_sw": _pad_heads(uq, [zeros_nope, uq[..., MLA_NOPE:][..., perm]]).astype(BF),
        "w_uk": _pad_heads(ukv, [ukv[..., :MLA_NOPE]]).astype(BF),
        "w_uv": ukv[..., MLA_NOPE:].reshape(MLA_KV_LORA, MLA_HEADS * MLA_V).astype(BF),
        "w_branch": w_branch[l].astype(BF),
        "w_out": w_out[l].astype(BF),
        "pool_w": pool_w[l].astype(BF),
        "pool_scale": pool_scale[l][None, :],
    }


def kernel(x_prompt, x_sample, cache_na_k, cache_na_v, cache_mla_ckv, cache_mla_krope, c, c_ctx, w_ada, b_ada, norm_pre, norm_post, ffn_w_in, ffn_w_out, w_in, na_rpb, pool_w, pool_scale, mla_q_norm, mla_kv_norm, mla_w_uq, mla_w_ukv, w_branch, w_out):
    batch, seq, _ = x_prompt.shape
    dec_batch, dec_seq, _ = x_sample.shape
    past = cache_na_k.shape[2]
    rows = dec_seq // GRID_W

    n_rows = -(-(1 + dec_batch) // 8) * 8
    cvec = jnp.concatenate([c_ctx[None, :], c, jnp.zeros((n_rows - 1 - dec_batch, D_MODEL), F32)], axis=0)
    mods = _modulations(cvec, w_ada, b_ada)

    lw = [_layer_weights(l, w_in, pool_w, pool_scale, mla_q_norm, mla_kv_norm, mla_w_uq, mla_w_ukv,
                         w_branch, w_out) for l in range(DEPTH)]
    rope = _rope_tables(dec_seq)
    bias = _na_bias_tables(na_rpb, rows)
    ffn_in = ffn_w_in.astype(BF)
    ffn_out = ffn_w_out.astype(BF)

    kc_na = cache_na_k.reshape(dec_batch, DEPTH, past, 512).transpose(1, 0, 2, 3).astype(BF)
    vc_na = cache_na_v.reshape(dec_batch, DEPTH, past, 512).transpose(1, 0, 2, 3).astype(BF)
    ckv_c = cache_mla_ckv.transpose(1, 0, 2, 3)
    kr_c = jnp.pad(cache_mla_krope.transpose(1, 0, 2, 3),
                   ((0, 0), (0, 0), (0, 0), (MLA_NOPE, MLA_QK_PAD - MLA_NOPE - MLA_ROPE)))
    kc_mla, vc_mla = _cache_kv(ckv_c, kr_c, jnp.stack([w["w_uk"] for w in lw]), jnp.stack([w["w_uv"] for w in lw]))

    xp = x_prompt.reshape(1, batch * seq, D_MODEL)
    xs = x_sample
    states = [[], [], [], []]
    tm = 512
    for l in range(DEPTH):
        m_ctx, m_lat = mods[l, 0:1], mods[l, 1:1 + dec_batch]
        nrm = lambda j: jnp.stack([norm_pre[l, j], norm_post[l, j]])
        w = lw[l]
        xp = _ffn(xp, m_ctx[:, 0:3], nrm(0), ffn_in[l, 0], ffn_out[l, 0], tm)
        xs = _ffn(xs, m_lat[:, 0:3], nrm(0), ffn_in[l, 0], ffn_out[l, 0], tm)

        q, k, v, pool_in, qm, km, vm, ckv, kr = _inproj(
            xp, m_ctx[:, 3:6], norm_pre[l, 1][None, :], w, None, tm, latent=False)
        per_seq = lambda a: a.reshape(batch, seq, a.shape[-1])
        na_o, mla_o = _ctx_attn(*(per_seq(a) for a in (q, k, v, qm, km, vm)))
        xp = _merge(xp.reshape(batch, seq, D_MODEL), jnp.broadcast_to(m_ctx[:, 3:6], (batch, 3, D_MODEL)),
                    nrm(1), na_o, mla_o, per_seq(pool_in), w, seq).reshape(1, batch * seq, D_MODEL)
        states[0].append(k.reshape(batch, seq, NA_HEADS, NA_HEAD_DIM))
        states[1].append(v.reshape(batch, seq, NA_HEADS, NA_HEAD_DIM))
        states[2].append(ckv.reshape(batch, seq, MLA_KV_LORA))
        states[3].append(kr.reshape(batch, seq, LANES)[..., MLA_NOPE:MLA_NOPE + MLA_ROPE])

        q, k, v, pool_in, qm, km, vm = _inproj(xs, m_lat[:, 3:6], norm_pre[l, 1][None, :], w, rope, tm, latent=True)
        na_o = _lat_na(q, k, v, kc_na[l], vc_na[l], bias, l)
        mla_o = _lat_mla(qm, km, vm, kc_mla[l], vc_mla[l], 256)
        xs = _merge(xs, m_lat[:, 3:6], nrm(1), na_o, mla_o, pool_in, w, tm)

        xp = _ffn(xp, m_ctx[:, 6:9], nrm(2), ffn_in[l, 1], ffn_out[l, 1], tm)
        xs = _ffn(xs, m_lat[:, 6:9], nrm(2), ffn_in[l, 1], ffn_out[l, 1], tm)

    return (xp.reshape(batch, seq, D_MODEL), xs) + tuple(jnp.stack(s, axis=1) for s in states)
```

```python
import functools

import numpy as np
import jax
import jax.numpy as jnp
from jax import lax
from jax.experimental import pallas as pl
from jax.experimental.pallas import tpu as pltpu

D_MODEL = 1024
DEPTH = 2
GRID_W = 64
N_MOD = 9
FFN_DIM = 2816
BRANCH_W = 512
NA_HEADS = 8
NA_HEAD_DIM = 64
NA_WIN_H = 8
NA_WIN_W = 16
POOL_GROUPS = 4
POOL_GROUP_DIM = 128
POOL_WINDOWS = (2, 4, 8, 16)
POOL_HALO = 8
MLA_HEADS = 8
MLA_Q_LORA = 256
MLA_KV_LORA = 256
MLA_NOPE = 64
MLA_ROPE = 32
MLA_V = 64
ROPE_BASE = 10000.0
NORM_EPS = 1e-6
LOG2E = 1.4426950408889634
NEG_INF = -1e30

LANES = 128
MLA_QK_PAD = LANES
VMEM_LIMIT_BYTES = 56 * 1024 * 1024

NA_QROWS = 4
NA_KROWS = 12
NA_KCHUNK = 512

BF = jnp.bfloat16
F32 = jnp.float32

_NT = (((1,), (1,)), ((), ()))


def _dot(a, b):
    return jnp.dot(a, b, preferred_element_type=F32)


def _dot_nt(a, b):
    return lax.dot_general(a, b, _NT, preferred_element_type=F32)


def _sigmoid(x):
    return 1.0 / (1.0 + jnp.exp(-x))


def _silu(x):
    return x * _sigmoid(x)


def _rms(x, g):
    ms = jnp.mean(x * x, axis=-1, keepdims=True)
    return (x * lax.rsqrt(ms + NORM_EPS)) * g


def _const_spec(shape):
    nd = len(shape)
    return pl.BlockSpec(shape, lambda *_: (0,) * nd, pipeline_mode=pl.Buffered(1))


def _params(*sem):
    return pltpu.CompilerParams(dimension_semantics=sem, vmem_limit_bytes=VMEM_LIMIT_BYTES)


def _mods_kernel(c_ref, w_ref, b_ref, o_ref):
    a = _silu(c_ref[...]).astype(BF)
    o_ref[0] = _dot(a, w_ref[0].astype(BF)) + b_ref[0]


def _modulations(cvec, w_ada, b_ada):
    r = cvec.shape[0]
    ncol = N_MOD * D_MODEL
    tn = 1024
    out = pl.pallas_call(
        _mods_kernel,
        out_shape=jax.ShapeDtypeStruct((DEPTH, r, ncol), F32),
        grid=(DEPTH, ncol // tn),
        in_specs=[
            pl.BlockSpec((r, D_MODEL), lambda l, j: (0, 0)),
            pl.BlockSpec((1, D_MODEL, tn), lambda l, j: (l, 0, j)),
            pl.BlockSpec((1, 1, tn), lambda l, j: (l, 0, j)),
        ],
        out_specs=pl.BlockSpec((1, r, tn), lambda l, j: (l, 0, j)),
        compiler_params=_params("arbitrary", "arbitrary"),
        name="mods",
    )(cvec, w_ada, b_ada.reshape(DEPTH, 1, ncol))
    return out.reshape(DEPTH, r, N_MOD, D_MODEL)


_FFN_CHUNKS = ((0, 512), (512, 512), (1024, 512), (1536, 512), (2048, 512), (2560, 256))


def _ffn_kernel(x_ref, mod_ref, nrm_ref, win_ref, wout_ref, o_ref):
    x = x_ref[0]
    shift, scale, gate = mod_ref[0, 0:1, :], mod_ref[0, 1:2, :], mod_ref[0, 2:3, :]
    u = (_rms(x, nrm_ref[0:1, :]) * (1.0 + scale) + shift).astype(BF)
    acc = None
    for off, ck in _FFN_CHUNKS:
        g = _dot(u, win_ref[:, off:off + ck])
        up = _dot(u, win_ref[:, FFN_DIM + off:FFN_DIM + off + ck])
        d = _dot((_silu(g) * up).astype(BF), wout_ref[off:off + ck, :])
        acc = d if acc is None else acc + d
    o_ref[0] = x + 0.5 * (gate * _rms(acc, nrm_ref[1:2, :]))


def _ffn(x, mods3, nrm2, w_in, w_out, tm):
    b, n, _ = x.shape
    return pl.pallas_call(
        _ffn_kernel,
        out_shape=jax.ShapeDtypeStruct(x.shape, F32),
        grid=(b, n // tm),
        in_specs=[
            pl.BlockSpec((1, tm, D_MODEL), lambda i, j: (i, j, 0)),
            pl.BlockSpec((1, 3, D_MODEL), lambda i, j: (i, 0, 0)),
            _const_spec((2, D_MODEL)),
            _const_spec((D_MODEL, 2 * FFN_DIM)),
            _const_spec((FFN_DIM, D_MODEL)),
        ],
        out_specs=pl.BlockSpec((1, tm, D_MODEL), lambda i, j: (i, j, 0)),
        compiler_params=_params("arbitrary", "arbitrary"),
        name="ffn",
    )(x, mods3, nrm2, w_in, w_out)


_C_Q, _C_K, _C_V, _C_POOL, _C_CQ, _C_CKV, _C_KR, _C_KRS, _C_END = (
    0, 512, 1024, 1536, 2048, 2304, 2560, 2688, 2816)


def _inproj_kernel(*refs, latent):
    if latent:
        (x_ref, mod_ref, g_ref, w1_ref, qn_ref, kvn_ref, wuq_ref, wuqs_ref, wuk_ref, wuv_ref,
         cos_ref, sin_ref,
         q_ref, k_ref, v_ref, pool_ref, qm_ref, km_ref, vm_ref) = refs
    else:
        (x_ref, mod_ref, g_ref, w1_ref, qn_ref, kvn_ref, wuq_ref, wuqs_ref, wuk_ref, wuv_ref,
         q_ref, k_ref, v_ref, pool_ref, qm_ref, km_ref, vm_ref, ckv_ref, kr_ref) = refs
    x = x_ref[0]
    shift, scale = mod_ref[0, 0:1, :], mod_ref[0, 1:2, :]
    u = (_rms(x, g_ref[...]) * (1.0 + scale) + shift).astype(BF)
    h = _dot(u, w1_ref[...])
    q_ref[0] = (h[:, _C_Q:_C_K] * (NA_HEAD_DIM ** -0.5 * LOG2E)).astype(q_ref.dtype)
    k_ref[0] = h[:, _C_K:_C_V].astype(k_ref.dtype)
    v_ref[0] = h[:, _C_V:_C_POOL].astype(v_ref.dtype)
    pool_ref[0] = h[:, _C_POOL:_C_CQ]

    qn = _rms(h[:, _C_CQ:_C_CKV], qn_ref[...]).astype(BF)
    ckv = _rms(h[:, _C_CKV:_C_KR], kvn_ref[...])
    ckv_b = ckv.astype(BF)
    kr = h[:, _C_KR:_C_KRS]
    qa = _dot(qn, wuq_ref[...])
    if latent:
        cos, sin = cos_ref[...], sin_ref[...]
        qb = _dot(qn, wuqs_ref[...])
        kr = kr * cos + h[:, _C_KRS:_C_END] * sin
    else:
        ckv_ref[0] = ckv
        kr_ref[0] = kr
    kn = _dot(ckv_b, wuk_ref[...])
    if latent:
        vm_ref[0] = _dot_nt(wuv_ref[...], ckv_b).astype(BF)
    else:
        vm_ref[0] = _dot(ckv_b, wuv_ref[...]).astype(BF)
    qscale =(MLA_NOPE + MLA_ROPE) ** -0.5 * LOG2E
    for hd in range(MLA_HEADS):
        sl = slice(hd * MLA_QK_PAD, (hd + 1) * MLA_QK_PAD)
        qh = qa[:, sl]
        if latent:
            qh = qh * cos + qb[:, sl] * sin
        qm_ref[0, :, sl] = (qh * qscale).astype(BF)
        km_ref[0, :, sl] = (kn[:, sl] + kr).astype(BF)


def _inproj(x, mods3, g_pre, wts, rope, tm, latent):
    b, n, _ = x.shape
    tok = lambda w, dt: jax.ShapeDtypeStruct((b, n, w), dt)
    tspec = lambda w: pl.BlockSpec((1, tm, w), lambda i, j: (i, j, 0))
    kv_dt = BF if latent else F32
    nv = MLA_HEADS * MLA_V
    out_shape = [tok(512, BF), tok(512, kv_dt), tok(512, kv_dt), tok(512, F32), tok(1024, BF), tok(1024, BF),
                 jax.ShapeDtypeStruct((b, nv, n), BF) if latent else tok(nv, BF)]
    out_specs = [tspec(512), tspec(512), tspec(512), tspec(512), tspec(1024), tspec(1024),
                 pl.BlockSpec((1, nv, tm), lambda i, j: (i, 0, j)) if latent else tspec(nv)]
    in_specs = [
        tspec(D_MODEL),
        pl.BlockSpec((1, 3, D_MODEL), lambda i, j: (i, 0, 0)),
        _const_spec((1, D_MODEL)),
        _const_spec((D_MODEL, _C_END)),
        _const_spec((1, MLA_Q_LORA)),
        _const_spec((1, MLA_KV_LORA)),
        _const_spec((MLA_Q_LORA, MLA_HEADS * MLA_QK_PAD)),
        _const_spec((MLA_Q_LORA, MLA_HEADS * MLA_QK_PAD)),
        _const_spec((MLA_KV_LORA, MLA_HEADS * MLA_QK_PAD)),
        _const_spec((nv, MLA_KV_LORA) if latent else (MLA_KV_LORA, nv)),
    ]
    args = [x, mods3, g_pre, wts["w1"], wts["q_norm"], wts["kv_norm"], wts["w_uq"], wts["w_uq_sw"],
            wts["w_uk"], wts["w_uv_t"] if latent else wts["w_uv"]]
    if latent:
        in_specs += [pl.BlockSpec((tm, LANES), lambda i, j: (j, 0))] * 2
        args += [rope[0], rope[1]]
    else:
        out_shape += [tok(MLA_KV_LORA, F32), tok(LANES, F32)]
        out_specs += [tspec(MLA_KV_LORA), tspec(LANES)]
    return pl.pallas_call(
        functools.partial(_inproj_kernel, latent=latent),
        out_shape=out_shape,
        grid=(b, n // tm),
        in_specs=in_specs,
        out_specs=out_specs,
        compiler_params=_params("arbitrary", "arbitrary"),
        name="inproj_lat" if latent else "inproj_ctx",
    )(*args)


def _cache_kv_kernel(ckv_ref, kr_ref, wuk_ref, wuv_ref, km_ref, vm_ref):
    ckv = ckv_ref[0, 0].astype(BF)
    kr = kr_ref[0, 0]
    kn = _dot(ckv, wuk_ref[0])
    vm_ref[0, 0] = _dot_nt(wuv_ref[0], ckv).astype(BF)
    for hd in range(MLA_HEADS):
        sl = slice(hd * MLA_QK_PAD, (hd + 1) * MLA_QK_PAD)
        km_ref[0, 0, :, sl] = (kn[:, sl] + kr).astype(BF)


def _cache_kv(ckv_c, kr_pad_c, w_uk, w_uv_t):
    _, b, p, _ = ckv_c.shape
    nv = MLA_HEADS * MLA_V
    spec = lambda r, w: pl.BlockSpec((1, 1, r, w), lambda l, i: (l, i, 0, 0))
    wspec = lambda r, w: pl.BlockSpec((1, r, w), lambda l, i: (l, 0, 0))
    return pl.pallas_call(
        _cache_kv_kernel,
        out_shape=[jax.ShapeDtypeStruct((DEPTH, b, p, MLA_HEADS * MLA_QK_PAD), BF),
                   jax.ShapeDtypeStruct((DEPTH, b, nv, p), BF)],
        grid=(DEPTH, b),
        in_specs=[spec(p, MLA_KV_LORA), spec(p, LANES),
                  wspec(MLA_KV_LORA, MLA_HEADS * MLA_QK_PAD), wspec(nv, MLA_KV_LORA)],
        out_specs=[spec(p, MLA_HEADS * MLA_QK_PAD), spec(nv, p)],
        compiler_params=_params("arbitrary", "arbitrary"),
        name="cache_kv",
    )(ckv_c, kr_pad_c, w_uk, w_uv_t)


def _half_mask(width, j):
    lane = lax.broadcasted_iota(jnp.int32, (1, width), 1)
    return (lane // NA_HEAD_DIM) == j


def _softmax_pv(scores, values):
    m = None
    for s in scores:
        mi = jnp.max(s, axis=-1, keepdims=True)
        m = mi if m is None else jnp.maximum(m, mi)
    l = None
    o = None
    for s, v in zip(scores, values):
        p = jnp.exp2(s - m)
        li = jnp.sum(p, axis=-1, keepdims=True)
        oi = _dot(p.astype(BF), v)
        l = li if l is None else l + li
        o = oi if o is None else o + oi
    return o * (1.0 / l)


def _fold_lanes(x, op):
    out = x[:, 0:LANES]
    for t in range(1, x.shape[1] // LANES):
        out = op(out, x[:, t * LANES:(t + 1) * LANES])
    return out


def _fold_rows(x, op):
    out = x[0:8, :]
    for t in range(1, x.shape[0] // 8):
        out = op(out, x[t * 8:(t + 1) * 8, :])
    return out


def _pipelined_softmax(n_heads, n_chunks, key_axis, score_chunk, prob_chunk, value_chunk, emit):
    def combine(acc, x, op):
        return x if acc is None else op(acc, x)

    mrun = None
    for ci in range(n_chunks):
        mrun = combine(mrun, score_chunk(0, ci), jnp.maximum)
    m_next = jnp.max(mrun, axis=key_axis, keepdims=True)
    inv_l = None
    for it in range(n_heads + 1):
        m, mrun, lrun, pv = m_next, None, None, None
        for ci in range(n_chunks):
            if it + 1 < n_heads:
                mrun = combine(mrun, score_chunk(it + 1, ci), jnp.maximum)
            if it < n_heads:
                lrun = combine(lrun, prob_chunk(it, ci, m), jnp.add)
            if it >= 1:
                pv = combine(pv, value_chunk(it - 1, ci), jnp.add)
        if it + 1 < n_heads:
            m_next = jnp.max(mrun, axis=key_axis, keepdims=True)
        if it >= 1:
            emit(it - 1, pv * inv_l)
        if it < n_heads:
            inv_l = 1.0 / jnp.sum(lrun, axis=key_axis, keepdims=True)


def _ctx_attn_kernel(q_ref, k_ref, v_ref, qm_ref, km_ref, vm_ref, na_ref, mla_ref):
    zero = jnp.zeros((), BF)
    for hp in range(NA_HEADS // 2):
        sl = slice(hp * LANES, (hp + 1) * LANES)
        qblk = q_ref[0, :, sl]
        kblk = k_ref[0, :, sl].astype(BF)
        vblk = v_ref[0, :, sl].astype(BF)
        vmblk = vm_ref[0, :, sl]
        acc_na = None
        acc_mla = None
        for j in range(2):
            hm = _half_mask(LANES, j)
            s = _dot_nt(jnp.where(hm, qblk, zero), kblk)
            o = _softmax_pv([s], [jnp.where(hm, vblk, zero)])
            acc_na = o if acc_na is None else acc_na + o
            hd = 2 * hp + j
            hsl = slice(hd * MLA_QK_PAD, (hd + 1) * MLA_QK_PAD)
            s = _dot_nt(qm_ref[0, :, hsl], km_ref[0, :, hsl])
            o = _softmax_pv([s], [jnp.where(hm, vmblk, zero)])
            acc_mla = o if acc_mla is None else acc_mla + o
        na_ref[0, :, sl] = acc_na.astype(BF)
        mla_ref[0, :, sl] = acc_mla.astype(BF)


def _ctx_attn(q, k, v, qm, km, vm):
    b, s, _ = q.shape
    spec = lambda w: pl.BlockSpec((1, s, w), lambda i: (i, 0, 0))
    return pl.pallas_call(
        _ctx_attn_kernel,
        out_shape=[jax.ShapeDtypeStruct((b, s, BRANCH_W), BF)] * 2,
        grid=(b,),
        in_specs=[spec(512), spec(512), spec(512), spec(1024), spec(1024), spec(512)],
        out_specs=[spec(BRANCH_W), spec(BRANCH_W)],
        compiler_params=_params("arbitrary"),
        name="ctx_attn",
    )(q, k, v, qm, km, vm)


MLA_KCHUNK = 512


def _lat_mla_kernel(q_ref, k_ref, vt_ref, kc_ref, vct_ref, o_ref, s_ref, p_ref):
    n, pc = k_ref.shape[1], kc_ref.shape[1]
    chunks = [(k_ref, vt_ref, off, MLA_KCHUNK, off) for off in range(0, n, MLA_KCHUNK)] + [(kc_ref, vct_ref, 0, pc, n)]

    def score_chunk(hd, ci):
        kref, _, off, w, row = chunks[ci]
        hsl = slice(hd * MLA_QK_PAD, (hd + 1) * MLA_QK_PAD)
        s = _dot_nt(kref[0, off:off + w, hsl], q_ref[0, :, hsl])
        s_ref[hd % 2, row:row + w, :] = s
        return _fold_rows(s, jnp.maximum)

    def prob_chunk(hd, ci, m):
        _, _, _, w, row = chunks[ci]
        p = jnp.exp2(s_ref[hd % 2, row:row + w, :] - m)
        p_ref[hd % 2, row:row + w, :] = p.astype(BF)
        return _fold_rows(p, jnp.add)

    def value_chunk(hd, ci):
        _, vref, off, w, row = chunks[ci]
        return _dot(vref[0, hd * MLA_V:(hd + 1) * MLA_V, off:off + w], p_ref[hd % 2, row:row + w, :])

    pending = {}

    def emit(hd, o_t):
        if hd % 2 == 0:
            pending[hd] = o_t
        else:
            pair_t = jnp.concatenate([pending.pop(hd - 1), o_t], axis=0)
            o_ref[0, :, (hd // 2) * LANES:(hd // 2 + 1) * LANES] = pair_t.T.astype(BF)

    _pipelined_softmax(MLA_HEADS, len(chunks), 0, score_chunk, prob_chunk, value_chunk, emit)


def _lat_mla(qm, km, vm_t, kc, vc_t, tq):
    b, n, _ = qm.shape
    p = kc.shape[1]
    nv = MLA_HEADS * MLA_V
    return pl.pallas_call(
        _lat_mla_kernel,
        scratch_shapes=[pltpu.VMEM((2, n + p, tq), F32), pltpu.VMEM((2, n + p, tq), BF)],
        out_shape=jax.ShapeDtypeStruct((b, n, BRANCH_W), BF),
        grid=(b, n // tq),
        in_specs=[
            pl.BlockSpec((1, tq, 1024), lambda i, j: (i, j, 0)),
            pl.BlockSpec((1, n, 1024), lambda i, j: (i, 0, 0)),
            pl.BlockSpec((1, nv, n), lambda i, j: (i, 0, 0)),
            pl.BlockSpec((1, p, 1024), lambda i, j: (i, 0, 0)),
            pl.BlockSpec((1, nv, p), lambda i, j: (i, 0, 0)),
        ],
        out_specs=pl.BlockSpec((1, tq, BRANCH_W), lambda i, j: (i, j, 0)),
        compiler_params=_params("arbitrary", "arbitrary"),
        name="lat_mla",
    )(qm, km, vm_t, kc, vc_t)


_RPB_ROWS = 2 * NA_WIN_H - 1
_RPB_COLS = 2 * NA_WIN_W - 1


def _na_bias_kernel(rpb_ref, o_ref, *, rows):
    base = (pl.program_id(0) * NA_HEADS + pl.program_id(1)) * (_RPB_ROWS * _RPB_COLS)
    shape = (GRID_W, 2 * GRID_W)
    qc = lax.broadcasted_iota(jnp.int32, shape, 0)
    lane = lax.broadcasted_iota(jnp.int32, shape, 1)
    kc = lane & (GRID_W - 1)
    upper = lane >= GRID_W
    ws = jnp.clip(qc - NA_WIN_W // 2, 0, GRID_W - NA_WIN_W)
    in_cols = (kc >= ws) & (kc < ws + NA_WIN_W)
    cidx = kc - qc + (NA_WIN_W - 1)
    col_is = [cidx == j for j in range(_RPB_COLS)]
    neg = jnp.full(shape, NEG_INF, F32)
    tiles = {}

    def pair_tile(ia, ib):
        if (ia, ib) not in tiles:
            if ia is None and ib is None:
                tiles[(ia, ib)] = neg
            else:
                acc = neg
                for j in range(_RPB_COLS):
                    sa = NEG_INF if ia is None else rpb_ref[base + ia * _RPB_COLS + j] * LOG2E
                    sb = NEG_INF if ib is None else rpb_ref[base + ib * _RPB_COLS + j] * LOG2E
                    acc = jnp.where(col_is[j], jnp.where(upper, sb, sa), acc)
                valid = in_cols
                if ia is None:
                    valid = valid & upper
                if ib is None:
                    valid = valid & jnp.logical_not(upper)
                tiles[(ia, ib)] = jnp.where(valid, acc, neg)
        return tiles[(ia, ib)]

    for kind, r0 in enumerate((0, 2 * NA_QROWS, rows - NA_QROWS)):
        ks = int(np.clip(r0 - NA_WIN_H // 2, 0, rows - NA_KROWS))
        for qr in range(NA_QROWS):
            r = r0 + qr
            rs = int(np.clip(r - NA_WIN_H // 2, 0, rows - NA_WIN_H))
            for m in range(NA_KROWS // 2):
                idx = [(ks + krl - r + NA_WIN_H - 1) if rs <= ks + krl < rs + NA_WIN_H else None
                       for krl in (2 * m, 2 * m + 1)]
                o_ref[0, kind, 0, qr * GRID_W:(qr + 1) * GRID_W, m * 2 * GRID_W:(m + 1) * 2 * GRID_W] = (
                    pair_tile(*idx))


def _na_bias_tables(rpb, rows):
    tq, nk = NA_QROWS * GRID_W, NA_KROWS * GRID_W
    return pl.pallas_call(
        functools.partial(_na_bias_kernel, rows=rows),
        out_shape=jax.ShapeDtypeStruct((DEPTH, 3, NA_HEADS, tq, nk), F32),
        grid=(DEPTH, NA_HEADS),
        in_specs=[pl.BlockSpec(memory_space=pltpu.SMEM)],
        out_specs=pl.BlockSpec((1, 3, 1, tq, nk), lambda l, h: (l, 0, h, 0, 0)),
        compiler_params=_params("arbitrary", "arbitrary"),
        name="na_bias",
    )(rpb.reshape(-1))


def _lat_na_kernel(q_ref, k_ref, v_ref, kc_ref, vc_ref, bias_ref, o_ref, s_ref, p_ref, *, rows):
    g = pl.program_id(1)
    ks = jnp.clip(NA_QROWS * g - NA_WIN_H // 2, 0, rows - NA_KROWS)
    k0 = pl.multiple_of(ks * GRID_W, GRID_W)
    nk = NA_KROWS * GRID_W
    pc = kc_ref.shape[1]
    zero = jnp.zeros((), BF)
    local_chunks = [(off, min(NA_KCHUNK, nk - off)) for off in range(0, nk, NA_KCHUNK)]
    n_chunks = len(local_chunks) + 1

    def pair(hd):
        return slice((hd // 2) * LANES, (hd // 2 + 1) * LANES)

    def score_chunk(hd, ci):
        qh = jnp.where(_half_mask(LANES, hd % 2), q_ref[0, :, pair(hd)], zero)
        if ci < len(local_chunks):
            off, w = local_chunks[ci]
            kblk = k_ref[0, pl.ds(pl.multiple_of(k0 + off, GRID_W), w), pair(hd)]
            s = _dot_nt(qh, kblk) + bias_ref[0, hd, :, off:off + w]
        else:
            off, w = nk, pc
            s = _dot_nt(qh, kc_ref[0, :, pair(hd)])
        s_ref[hd % 2, :, off:off + w] = s
        return _fold_lanes(s, jnp.maximum)

    def prob_chunk(hd, ci, m):
        off, w = local_chunks[ci] if ci < len(local_chunks) else (nk, pc)
        p = jnp.exp2(s_ref[hd % 2, :, off:off + w] - m)
        p_ref[hd % 2, :, off:off + w] = p.astype(BF)
        return _fold_lanes(p, jnp.add)

    def value_chunk(hd, ci):
        if ci < len(local_chunks):
            off, w = local_chunks[ci]
            vblk = v_ref[0, pl.ds(pl.multiple_of(k0 + off, GRID_W), w), pair(hd)]
        else:
            off, w = nk, pc
            vblk = vc_ref[0, :, pair(hd)]
        return _dot(p_ref[hd % 2, :, off:off + w], jnp.where(_half_mask(LANES, hd % 2), vblk, zero))

    pending = {}

    def emit(hd, o):
        if hd % 2 == 0:
            pending[hd] = o
        else:
            o_ref[0, :, pair(hd)] = (pending.pop(hd - 1) + o).astype(BF)

    _pipelined_softmax(NA_HEADS, n_chunks, -1, score_chunk, prob_chunk, value_chunk, emit)


def _lat_na(q, k, v, kc, vc, bias, layer):
    b, n, _ = q.shape
    p = kc.shape[1]
    rows = n // GRID_W
    ngroups = rows // NA_QROWS
    tq = NA_QROWS * GRID_W
    nk = NA_KROWS * GRID_W

    def bias_map(i, g):
        kind = jnp.where(g == 0, 0, jnp.where(g == ngroups - 1, 2, 1))
        return (layer, kind, 0, 0, 0)

    return pl.pallas_call(
        functools.partial(_lat_na_kernel, rows=rows),
        out_shape=jax.ShapeDtypeStruct((b, n, BRANCH_W), BF),
        scratch_shapes=[pltpu.VMEM((2, tq, nk + p), F32), pltpu.VMEM((2, tq, nk + p), BF)],
        grid=(b, ngroups),
        in_specs=[
            pl.BlockSpec((1, tq, 512), lambda i, g: (i, g, 0)),
            pl.BlockSpec((1, n, 512), lambda i, g: (i, 0, 0)),
            pl.BlockSpec((1, n, 512), lambda i, g: (i, 0, 0)),
            pl.BlockSpec((1, p, 512), lambda i, g: (i, 0, 0)),
            pl.BlockSpec((1, p, 512), lambda i, g: (i, 0, 0)),
            pl.BlockSpec((None, 1, NA_HEADS, tq, nk), bias_map),
        ],
        out_specs=pl.BlockSpec((1, tq, BRANCH_W), lambda i, g: (i, g, 0)),
        compiler_params=_params("arbitrary", "arbitrary"),
        name="lat_na",
    )(q, k, v, kc, vc, bias)


def _merge_kernel(x_ref, mod_ref, nrm_ref, na_ref, mla_ref, pool_ref, prev_ref, next_ref,
                  wg_ref, wb_ref, wo_ref, pw_ref, ps_ref, o_ref, ext_ref, *, seq_len):
    tm = x_ref.shape[1]
    t = pl.program_id(1)
    nt = pl.num_programs(1)
    y_na = _dot(na_ref[0], wb_ref[0])
    y_mla = _dot(mla_ref[0], wb_ref[2])
    x = x_ref[0]
    shift, scale, gate = mod_ref[0, 0:1, :], mod_ref[0, 1:2, :], mod_ref[0, 2:3, :]
    u = (_rms(x, nrm_ref[0:1, :]) * (1.0 + scale) + shift).astype(BF)

    cur = pool_ref[0]
    ext_ref[0:POOL_HALO, :] = jnp.where(t > 0, prev_ref[0], 0.0)
    ext_ref[POOL_HALO:POOL_HALO + tm, :] = cur
    ext_ref[POOL_HALO + tm:POOL_HALO + tm + POOL_HALO, :] = jnp.where(t < nt - 1, next_ref[0], 0.0)
    pos = t * tm + lax.broadcasted_iota(jnp.int32, (tm, 1), 0)
    pooled = []
    for gi, w in enumerate(POOL_WINDOWS):
        sl = slice(gi * POOL_GROUP_DIM, (gi + 1) * POOL_GROUP_DIM)
        tot = None
        for j in range(-(w // 2), w // 2):
            piece = ext_ref[POOL_HALO + j:POOL_HALO + j + tm, sl]
            tot = piece if tot is None else tot + piece
        cnt = (jnp.minimum(pos + w // 2, seq_len) - jnp.maximum(pos - w // 2, 0)).astype(F32)
        diff = (tot / cnt - cur[:, sl]).astype(BF)
        pooled.append(_dot(diff, pw_ref[gi]))
    pool_o = (jnp.concatenate(pooled, axis=-1) * ps_ref[...]).astype(BF)

    ys = (y_na, _dot(pool_o, wb_ref[1]), y_mla)
    z = None
    for kb in range(3):
        gk = _sigmoid(_dot(u, wg_ref[:, kb * D_MODEL:(kb + 1) * D_MODEL]))
        z = gk * ys[kb] if z is None else z + gk * ys[kb]
    y = _dot(z.astype(BF), wo_ref[...])
    o_ref[0] = x + gate * _rms(y, nrm_ref[1:2, :])


def _merge(x, mods3, nrm2, na_o, mla_o, pool_in, wts, tm):
    b, n, _ = x.shape
    nh = tm // POOL_HALO
    last_h = n // POOL_HALO - 1
    tspec = lambda w: pl.BlockSpec((1, tm, w), lambda i, j: (i, j, 0))
    return pl.pallas_call(
        functools.partial(_merge_kernel, seq_len=n),
        out_shape=jax.ShapeDtypeStruct(x.shape, F32),
        grid=(b, n // tm),
        in_specs=[
            tspec(D_MODEL),
            pl.BlockSpec((1, 3, D_MODEL), lambda i, j: (i, 0, 0)),
            _const_spec((2, D_MODEL)),
            tspec(BRANCH_W), tspec(BRANCH_W), tspec(512),
            pl.BlockSpec((1, POOL_HALO, 512), lambda i, j: (i, jnp.maximum(j * nh - 1, 0), 0)),
            pl.BlockSpec((1, POOL_HALO, 512), lambda i, j: (i, jnp.minimum((j + 1) * nh, last_h), 0)),
            _const_spec((D_MODEL, 3 * D_MODEL)),
            _const_spec((3, BRANCH_W, D_MODEL)),
            _const_spec((D_MODEL, D_MODEL)),
            _const_spec((POOL_GROUPS, POOL_GROUP_DIM, POOL_GROUP_DIM)),
            _const_spec((1, 512)),
        ],
        out_specs=tspec(D_MODEL),
        scratch_shapes=[pltpu.VMEM((tm + 2 * POOL_HALO, 512), F32)],
        compiler_params=_params("arbitrary", "arbitrary"),
        name="merge",
    )(x, mods3, nrm2, na_o, mla_o, pool_in, pool_in, pool_in,
      wts["w_g"], wts["w_branch"], wts["w_out"], wts["pool_w"], wts["pool_scale"])


def _rope_tables(n_tok):
    half = MLA_ROPE // 2
    n_freq = half // 2
    inv = ROPE_BASE ** (-jnp.arange(n_freq, dtype=F32) / n_freq)
    t = jnp.arange(n_tok)
    ang_r = (t // GRID_W).astype(F32)[:, None] * inv
    ang_c = (t % GRID_W).astype(F32)[:, None] * inv
    ones = jnp.ones((n_tok, MLA_NOPE), F32)
    tail = jnp.zeros((n_tok, MLA_QK_PAD - MLA_NOPE - MLA_ROPE), F32)
    cos = jnp.concatenate([ones, jnp.cos(ang_r), jnp.cos(ang_r), jnp.cos(ang_c), jnp.cos(ang_c), tail + 1.0], axis=-1)
    sin = jnp.concatenate([0.0 * ones, -jnp.sin(ang_r), jnp.sin(ang_r), -jnp.sin(ang_c), jnp.sin(ang_c), tail], axis=-1)
    return cos, sin


def _rope_partner_perm():
    n_freq = MLA_ROPE // 4
    idx = np.arange(MLA_ROPE)
    return np.where((idx // n_freq) % 2 == 0, idx + n_freq, idx - n_freq)


def _pad_heads(w, parts):
    r, h, _ = w.shape
    cols = jnp.concatenate(parts, axis=-1)
    pad = MLA_QK_PAD - cols.shape[-1]
    return jnp.pad(cols, ((0, 0), (0, 0), (0, pad))).reshape(r, h * MLA_QK_PAD)


def _layer_weights(l, w_in, pool_w, pool_scale, q_norm, kv_norm, w_uq, w_ukv, w_branch, w_out):
    perm = _rope_partner_perm()
    bounds = np.cumsum([512, 512, 512, 512, MLA_Q_LORA, MLA_KV_LORA, MLA_ROPE])
    wi = w_in[l]
    kr_cols = wi[:, bounds[5]:bounds[6]]
    lane_pad = lambda c: jnp.pad(c, ((0, 0), (MLA_NOPE, MLA_QK_PAD - MLA_NOPE - MLA_ROPE)))
    w1 = jnp.concatenate([wi[:, :bounds[5]], lane_pad(kr_cols), lane_pad(kr_cols[:, perm])], axis=-1).astype(BF)
    uq = w_uq[l].reshape(MLA_Q_LORA, MLA_HEADS, MLA_NOPE + MLA_ROPE)
    ukv = w_ukv[l].reshape(MLA_KV_LORA, MLA_HEADS, MLA_NOPE + MLA_V)
    zeros_nope = jnp.zeros((MLA_Q_LORA, MLA_HEADS, MLA_NOPE), F32)
    return {
        "w1": w1,
        "w_g": wi[:, bounds[6]:].astype(BF),
        "q_norm": q_norm[l][None, :],
        "kv_norm": kv_norm[l][None, :],
        "w_uq": _pad_heads(uq, [uq]).astype(BF),
        "w_uq_sw": _pad_heads(uq, [zeros_nope, uq[..., MLA_NOPE:][..., perm]]).astype(BF),
        "w_uk": _pad_heads(ukv, [ukv[..., :MLA_NOPE]]).astype(BF),
        "w_uv": ukv[..., MLA_NOPE:].reshape(MLA_KV_LORA, MLA_HEADS * MLA_V).astype(BF),
        "w_uv_t": ukv[..., MLA_NOPE:].reshape(MLA_KV_LORA, MLA_HEADS * MLA_V).T.astype(BF),
        "w_branch": w_branch[l].astype(BF),
        "w_out": w_out[l].astype(BF),
        "pool_w": pool_w[l].astype(BF),
        "pool_scale": pool_scale[l][None, :],
    }


def kernel(x_prompt, x_sample, cache_na_k, cache_na_v, cache_mla_ckv, cache_mla_krope, c, c_ctx, w_ada, b_ada, norm_pre, norm_post, ffn_w_in, ffn_w_out, w_in, na_rpb, pool_w, pool_scale, mla_q_norm, mla_kv_norm, mla_w_uq, mla_w_ukv, w_branch, w_out):
    batch, seq, _ = x_prompt.shape
    dec_batch, dec_seq, _ = x_sample.shape
    past = cache_na_k.shape[2]
    rows = dec_seq // GRID_W

    n_rows = -(-(1 + dec_batch) // 8) * 8
    cvec = jnp.concatenate([c_ctx[None, :], c, jnp.zeros((n_rows - 1 - dec_batch, D_MODEL), F32)], axis=0)
    mods = _modulations(cvec, w_ada, b_ada)

    lw = [_layer_weights(l, w_in, pool_w, pool_scale, mla_q_norm, mla_kv_norm, mla_w_uq, mla_w_ukv,
                         w_branch, w_out) for l in range(DEPTH)]
    rope = _rope_tables(dec_seq)
    bias = _na_bias_tables(na_rpb, rows)
    ffn_in = ffn_w_in.astype(BF)
    ffn_out = ffn_w_out.astype(BF)

    kc_na = cache_na_k.reshape(dec_batch, DEPTH, past, 512).transpose(1, 0, 2, 3).astype(BF)
    vc_na = cache_na_v.reshape(dec_batch, DEPTH, past, 512).transpose(1, 0, 2, 3).astype(BF)
    ckv_c = cache_mla_ckv.transpose(1, 0, 2, 3)
    kr_c = jnp.pad(cache_mla_krope.transpose(1, 0, 2, 3),
                   ((0, 0), (0, 0), (0, 0), (MLA_NOPE, MLA_QK_PAD - MLA_NOPE - MLA_ROPE)))
    kc_mla, vc_mla = _cache_kv(ckv_c, kr_c, jnp.stack([w["w_uk"] for w in lw]), jnp.stack([w["w_uv_t"] for w in lw]))

    xp = x_prompt.reshape(1, batch * seq, D_MODEL)
    xs = x_sample
    states = [[], [], [], []]
    tm = 512
    for l in range(DEPTH):
        m_ctx, m_lat = mods[l, 0:1], mods[l, 1:1 + dec_batch]
        nrm = lambda j: jnp.stack([norm_pre[l, j], norm_post[l, j]])
        w = lw[l]
        xp = _ffn(xp, m_ctx[:, 0:3], nrm(0), ffn_in[l, 0], ffn_out[l, 0], tm)
        xs = _ffn(xs, m_lat[:, 0:3], nrm(0), ffn_in[l, 0], ffn_out[l, 0], tm)

        q, k, v, pool_in, qm, km, vm, ckv, kr = _inproj(
            xp, m_ctx[:, 3:6], norm_pre[l, 1][None, :], w, None, tm, latent=False)
        per_seq = lambda a: a.reshape(batch, seq, a.shape[-1])
        na_o, mla_o = _ctx_attn(*(per_seq(a) for a in (q, k, v, qm, km, vm)))
        xp = _merge(xp.reshape(batch, seq, D_MODEL), jnp.broadcast_to(m_ctx[:, 3:6], (batch, 3, D_MODEL)),
                    nrm(1), na_o, mla_o, per_seq(pool_in), w, seq).reshape(1, batch * seq, D_MODEL)
        states[0].append(k.reshape(batch, seq, NA_HEADS, NA_HEAD_DIM))
        states[1].append(v.reshape(batch, seq, NA_HEADS, NA_HEAD_DIM))
        states[2].append(ckv.reshape(batch, seq, MLA_KV_LORA))
        states[3].append(kr.reshape(batch, seq, LANES)[..., MLA_NOPE:MLA_NOPE + MLA_ROPE])

        q, k, v, pool_in, qm, km, vm = _inproj(xs, m_lat[:, 3:6], norm_pre[l, 1][None, :], w, rope, tm, latent=True)
        na_o = _lat_na(q, k, v, kc_na[l], vc_na[l], bias, l)
        mla_o = _lat_mla(qm, km, vm, kc_mla[l], vc_mla[l], 256)
        xs = _merge(xs, m_lat[:, 3:6], nrm(1), na_o, mla_o, pool_in, w, tm)

        xp = _ffn(xp, m_ctx[:, 6:9], nrm(2), ffn_in[l, 1], ffn_out[l, 1], tm)
        xs = _ffn(xs, m_lat[:, 6:9], nrm(2), ffn_in[l, 1], ffn_out[l, 1], tm)

    return (xp.reshape(batch, seq, D_MODEL), xs) + tuple(jnp.stack(s, axis=1) for s in states)
```

```python
import functools

import numpy as np
import jax
import jax.numpy as jnp
from jax import lax
from jax.experimental import pallas as pl
from jax.experimental.pallas import tpu as pltpu

D_MODEL = 1024
DEPTH = 2
GRID_W = 64
N_MOD = 9
FFN_DIM = 2816
BRANCH_W = 512
NA_HEADS = 8
NA_HEAD_DIM = 64
NA_WIN_H = 8
NA_WIN_W = 16
POOL_GROUPS = 4
POOL_GROUP_DIM = 128
POOL_WINDOWS = (2, 4, 8, 16)
POOL_HALO = 8
MLA_HEADS = 8
MLA_Q_LORA = 256
MLA_KV_LORA = 256
MLA_NOPE = 64
MLA_ROPE = 32
MLA_V = 64
ROPE_BASE = 10000.0
NORM_EPS = 1e-6
LOG2E = 1.4426950408889634
NEG_INF = -1e30

LANES = 128
MLA_QK_PAD = LANES
VMEM_LIMIT_BYTES = 56 * 1024 * 1024

NA_QROWS = 4
NA_KROWS = 12
NA_KCHUNK = 512

BF = jnp.bfloat16
F32 = jnp.float32

_NT = (((1,), (1,)), ((), ()))


def _dot(a, b):
    return jnp.dot(a, b, preferred_element_type=F32)


def _dot_nt(a, b):
    return lax.dot_general(a, b, _NT, preferred_element_type=F32)


def _sigmoid(x):
    return 1.0 / (1.0 + jnp.exp(-x))


def _silu(x):
    return x * _sigmoid(x)


def _rms(x, g):
    ms = jnp.mean(x * x, axis=-1, keepdims=True)
    return (x * lax.rsqrt(ms + NORM_EPS)) * g


def _const_spec(shape):
    nd = len(shape)
    return pl.BlockSpec(shape, lambda *_: (0,) * nd, pipeline_mode=pl.Buffered(1))


def _params(*sem):
    return pltpu.CompilerParams(dimension_semantics=sem, vmem_limit_bytes=VMEM_LIMIT_BYTES)


def _mods_kernel(c_ref, w_ref, b_ref, o_ref):
    a = _silu(c_ref[...]).astype(BF)
    o_ref[0] = _dot(a, w_ref[0].astype(BF)) + b_ref[0]


def _modulations(cvec, w_ada, b_ada):
    r = cvec.shape[0]
    ncol = N_MOD * D_MODEL
    tn = 1024
    out = pl.pallas_call(
        _mods_kernel,
        out_shape=jax.ShapeDtypeStruct((DEPTH, r, ncol), F32),
        grid=(DEPTH, ncol // tn),
        in_specs=[
            pl.BlockSpec((r, D_MODEL), lambda l, j: (0, 0)),
            pl.BlockSpec((1, D_MODEL, tn), lambda l, j: (l, 0, j)),
            pl.BlockSpec((1, 1, tn), lambda l, j: (l, 0, j)),
        ],
        out_specs=pl.BlockSpec((1, r, tn), lambda l, j: (l, 0, j)),
        compiler_params=_params("arbitrary", "arbitrary"),
        name="mods",
    )(cvec, w_ada, b_ada.reshape(DEPTH, 1, ncol))
    return out.reshape(DEPTH, r, N_MOD, D_MODEL)


_FFN_CHUNKS = ((0, 512), (512, 512), (1024, 512), (1536, 512), (2048, 512), (2560, 256))


def _ffn_kernel(x_ref, mod_ref, nrm_ref, win_ref, wout_ref, o_ref):
    x = x_ref[0]
    shift, scale, gate = mod_ref[0, 0:1, :], mod_ref[0, 1:2, :], mod_ref[0, 2:3, :]
    u = (_rms(x, nrm_ref[0:1, :]) * (1.0 + scale) + shift).astype(BF)
    acc = None
    for off, ck in _FFN_CHUNKS:
        g = _dot(u, win_ref[:, off:off + ck])
        up = _dot(u, win_ref[:, FFN_DIM + off:FFN_DIM + off + ck])
        d = _dot((_silu(g) * up).astype(BF), wout_ref[off:off + ck, :])
        acc = d if acc is None else acc + d
    o_ref[0] = x + 0.5 * (gate * _rms(acc, nrm_ref[1:2, :]))


def _ffn(x, mods3, nrm2, w_in, w_out, tm):
    b, n, _ = x.shape
    return pl.pallas_call(
        _ffn_kernel,
        out_shape=jax.ShapeDtypeStruct(x.shape, F32),
        grid=(b, n // tm),
        in_specs=[
            pl.BlockSpec((1, tm, D_MODEL), lambda i, j: (i, j, 0)),
            pl.BlockSpec((1, 3, D_MODEL), lambda i, j: (i, 0, 0)),
            _const_spec((2, D_MODEL)),
            _const_spec((D_MODEL, 2 * FFN_DIM)),
            _const_spec((FFN_DIM, D_MODEL)),
        ],
        out_specs=pl.BlockSpec((1, tm, D_MODEL), lambda i, j: (i, j, 0)),
        compiler_params=_params("arbitrary", "arbitrary"),
        name="ffn",
    )(x, mods3, nrm2, w_in, w_out)


_C_Q, _C_K, _C_V, _C_POOL, _C_CQ, _C_CKV, _C_KR, _C_KRS, _C_END = (
    0, 512, 1024, 1536, 2048, 2304, 2560, 2688, 2816)


def _inproj_kernel(*refs, latent):
    if latent:
        (x_ref, mod_ref, g_ref, w1_ref, qn_ref, kvn_ref, wuq_ref, wuqs_ref, wuk_ref, wuv_ref,
         cos_ref, sin_ref,
         q_ref, k_ref, v_ref, pool_ref, qm_ref, km_ref, vm_ref) = refs
    else:
        (x_ref, mod_ref, g_ref, w1_ref, qn_ref, kvn_ref, wuq_ref, wuqs_ref, wuk_ref, wuv_ref,
         q_ref, k_ref, v_ref, pool_ref, qm_ref, km_ref, vm_ref, ckv_ref, kr_ref) = refs
    x = x_ref[0]
    shift, scale = mod_ref[0, 0:1, :], mod_ref[0, 1:2, :]
    u = (_rms(x, g_ref[...]) * (1.0 + scale) + shift).astype(BF)
    h = _dot(u, w1_ref[...])
    q_ref[0] = (h[:, _C_Q:_C_K] * (NA_HEAD_DIM ** -0.5 * LOG2E)).astype(q_ref.dtype)
    k_ref[0] = h[:, _C_K:_C_V].astype(k_ref.dtype)
    if latent:
        v_t = h[:, _C_V:_C_POOL].T.astype(BF)
        for t in range(v_ref.shape[1]):
            v_ref[0, t] = v_t[:, t * LANES:(t + 1) * LANES]
    else:
        v_ref[0] = h[:, _C_V:_C_POOL]
    pool_ref[0] = h[:, _C_POOL:_C_CQ]

    qn = _rms(h[:, _C_CQ:_C_CKV], qn_ref[...]).astype(BF)
    ckv = _rms(h[:, _C_CKV:_C_KR], kvn_ref[...])
    ckv_b = ckv.astype(BF)
    kr = h[:, _C_KR:_C_KRS]
    qa = _dot(qn, wuq_ref[...])
    if latent:
        cos, sin = cos_ref[...], sin_ref[...]
        qb = _dot(qn, wuqs_ref[...])
        kr = kr * cos + h[:, _C_KRS:_C_END] * sin
    else:
        ckv_ref[0] = ckv
        kr_ref[0] = kr
    kn = _dot(ckv_b, wuk_ref[...])
    if latent:
        vm_ref[0] = _dot_nt(wuv_ref[...], ckv_b).astype(BF)
    else:
        vm_ref[0] = _dot(ckv_b, wuv_ref[...]).astype(BF)
    qscale =(MLA_NOPE + MLA_ROPE) ** -0.5 * LOG2E
    for hd in range(MLA_HEADS):
        sl = slice(hd * MLA_QK_PAD, (hd + 1) * MLA_QK_PAD)
        qh = qa[:, sl]
        if latent:
            qh = qh * cos + qb[:, sl] * sin
        qm_ref[0, :, sl] = (qh * qscale).astype(BF)
        km_ref[0, :, sl] = (kn[:, sl] + kr).astype(BF)


def _inproj(x, mods3, g_pre, wts, rope, tm, latent):
    b, n, _ = x.shape
    tok = lambda w, dt: jax.ShapeDtypeStruct((b, n, w), dt)
    tspec = lambda w: pl.BlockSpec((1, tm, w), lambda i, j: (i, j, 0))
    kv_dt = BF if latent else F32
    nv = MLA_HEADS * MLA_V
    out_shape = [tok(512, BF), tok(512, kv_dt),
                 jax.ShapeDtypeStruct((b, n // LANES, 512, LANES), BF) if latent else tok(512, F32),
                 tok(512, F32), tok(1024, BF), tok(1024, BF),
                 jax.ShapeDtypeStruct((b, nv, n), BF) if latent else tok(nv, BF)]
    out_specs = [tspec(512), tspec(512),
                 pl.BlockSpec((1, tm // LANES, 512, LANES), lambda i, j: (i, j, 0, 0)) if latent else tspec(512),
                 tspec(512), tspec(1024), tspec(1024),
                 pl.BlockSpec((1, nv, tm), lambda i, j: (i, 0, j)) if latent else tspec(nv)]
    in_specs = [
        tspec(D_MODEL),
        pl.BlockSpec((1, 3, D_MODEL), lambda i, j: (i, 0, 0)),
        _const_spec((1, D_MODEL)),
        _const_spec((D_MODEL, _C_END)),
        _const_spec((1, MLA_Q_LORA)),
        _const_spec((1, MLA_KV_LORA)),
        _const_spec((MLA_Q_LORA, MLA_HEADS * MLA_QK_PAD)),
        _const_spec((MLA_Q_LORA, MLA_HEADS * MLA_QK_PAD)),
        _const_spec((MLA_KV_LORA, MLA_HEADS * MLA_QK_PAD)),
        _const_spec((nv, MLA_KV_LORA) if latent else (MLA_KV_LORA, nv)),
    ]
    args = [x, mods3, g_pre, wts["w1"], wts["q_norm"], wts["kv_norm"], wts["w_uq"], wts["w_uq_sw"],
            wts["w_uk"], wts["w_uv_t"] if latent else wts["w_uv"]]
    if latent:
        in_specs += [pl.BlockSpec((tm, LANES), lambda i, j: (j, 0))] * 2
        args += [rope[0], rope[1]]
    else:
        out_shape += [tok(MLA_KV_LORA, F32), tok(LANES, F32)]
        out_specs += [tspec(MLA_KV_LORA), tspec(LANES)]
    return pl.pallas_call(
        functools.partial(_inproj_kernel, latent=latent),
        out_shape=out_shape,
        grid=(b, n // tm),
        in_specs=in_specs,
        out_specs=out_specs,
        compiler_params=_params("arbitrary", "arbitrary"),
        name="inproj_lat" if latent else "inproj_ctx",
    )(*args)


def _cache_kv_kernel(ckv_ref, kr_ref, wuk_ref, wuv_ref, km_ref, vm_ref):
    ckv = ckv_ref[0, 0].astype(BF)
    kr = kr_ref[0, 0]
    kn = _dot(ckv, wuk_ref[0])
    vm_ref[0, 0] = _dot_nt(wuv_ref[0], ckv).astype(BF)
    for hd in range(MLA_HEADS):
        sl = slice(hd * MLA_QK_PAD, (hd + 1) * MLA_QK_PAD)
        km_ref[0, 0, :, sl] = (kn[:, sl] + kr).astype(BF)


def _cache_kv(ckv_c, kr_pad_c, w_uk, w_uv_t):
    b, _, p, _ = ckv_c.shape
    nv = MLA_HEADS * MLA_V
    in_spec = lambda r, w: pl.BlockSpec((1, 1, r, w), lambda l, i: (i, l, 0, 0))
    spec = lambda r, w: pl.BlockSpec((1, 1, r, w), lambda l, i: (l, i, 0, 0))
    wspec = lambda r, w: pl.BlockSpec((1, r, w), lambda l, i: (l, 0, 0))
    return pl.pallas_call(
        _cache_kv_kernel,
        out_shape=[jax.ShapeDtypeStruct((DEPTH, b, p, MLA_HEADS * MLA_QK_PAD), BF),
                   jax.ShapeDtypeStruct((DEPTH, b, nv, p), BF)],
        grid=(DEPTH, b),
        in_specs=[in_spec(p, MLA_KV_LORA), in_spec(p, LANES),
                  wspec(MLA_KV_LORA, MLA_HEADS * MLA_QK_PAD), wspec(nv, MLA_KV_LORA)],
        out_specs=[spec(p, MLA_HEADS * MLA_QK_PAD), spec(nv, p)],
        compiler_params=_params("arbitrary", "arbitrary"),
        name="cache_kv",
    )(ckv_c, kr_pad_c, w_uk, w_uv_t)


def _half_mask(width, j):
    lane = lax.broadcasted_iota(jnp.int32, (1, width), 1)
    return (lane // NA_HEAD_DIM) == j


def _softmax_pv(scores, values):
    m = None
    for s in scores:
        mi = jnp.max(s, axis=-1, keepdims=True)
        m = mi if m is None else jnp.maximum(m, mi)
    l = None
    o = None
    for s, v in zip(scores, values):
        p = jnp.exp2(s - m)
        li = jnp.sum(p, axis=-1, keepdims=True)
        oi = _dot(p.astype(BF), v)
        l = li if l is None else l + li
        o = oi if o is None else o + oi
    return o * (1.0 / l)


def _fold_rows(x, op):
    out = x[0:8, :]
    for t in range(1, x.shape[0] // 8):
        out = op(out, x[t * 8:(t + 1) * 8, :])
    return out


def _pipelined_softmax(n_heads, n_chunks, key_axis, score_chunk, prob_chunk, value_chunk, emit):
    def combine(acc, x, op):
        return x if acc is None else op(acc, x)

    mrun = None
    for ci in range(n_chunks):
        mrun = combine(mrun, score_chunk(0, ci), jnp.maximum)
    m_next = jnp.max(mrun, axis=key_axis, keepdims=True)
    inv_l = None
    for it in range(n_heads + 1):
        m, mrun, lrun, pv = m_next, None, None, None
        for ci in range(n_chunks):
            if it + 1 < n_heads:
                mrun = combine(mrun, score_chunk(it + 1, ci), jnp.maximum)
            if it < n_heads:
                lrun = combine(lrun, prob_chunk(it, ci, m), jnp.add)
            if it >= 1:
                pv = combine(pv, value_chunk(it - 1, ci), jnp.add)
        if it + 1 < n_heads:
            m_next = jnp.max(mrun, axis=key_axis, keepdims=True)
        if it >= 1:
            emit(it - 1, pv * inv_l)
        if it < n_heads:
            inv_l = 1.0 / jnp.sum(lrun, axis=key_axis, keepdims=True)


def _ctx_attn_kernel(q_ref, k_ref, v_ref, qm_ref, km_ref, vm_ref, na_ref, mla_ref):
    zero = jnp.zeros((), BF)
    for hp in range(NA_HEADS // 2):
        sl = slice(hp * LANES, (hp + 1) * LANES)
        qblk = q_ref[0, :, sl]
        kblk = k_ref[0, :, sl].astype(BF)
        vblk = v_ref[0, :, sl].astype(BF)
        vmblk = vm_ref[0, :, sl]
        acc_na = None
        acc_mla = None
        for j in range(2):
            hm = _half_mask(LANES, j)
            s = _dot_nt(jnp.where(hm, qblk, zero), kblk)
            o = _softmax_pv([s], [jnp.where(hm, vblk, zero)])
            acc_na = o if acc_na is None else acc_na + o
            hd = 2 * hp + j
            hsl = slice(hd * MLA_QK_PAD, (hd + 1) * MLA_QK_PAD)
            s = _dot_nt(qm_ref[0, :, hsl], km_ref[0, :, hsl])
            o = _softmax_pv([s], [jnp.where(hm, vmblk, zero)])
            acc_mla = o if acc_mla is None else acc_mla + o
        na_ref[0, :, sl] = acc_na.astype(BF)
        mla_ref[0, :, sl] = acc_mla.astype(BF)


def _ctx_attn(q, k, v, qm, km, vm):
    b, s, _ = q.shape
    spec = lambda w: pl.BlockSpec((1, s, w), lambda i: (i, 0, 0))
    return pl.pallas_call(
        _ctx_attn_kernel,
        out_shape=[jax.ShapeDtypeStruct((b, s, BRANCH_W), BF)] * 2,
        grid=(b,),
        in_specs=[spec(512), spec(512), spec(512), spec(1024), spec(1024), spec(512)],
        out_specs=[spec(BRANCH_W), spec(BRANCH_W)],
        compiler_params=_params("arbitrary"),
        name="ctx_attn",
    )(q, k, v, qm, km, vm)


MLA_KCHUNK = 512


def _lat_mla_kernel(q_ref, k_ref, vt_ref, kc_ref, vct_ref, o_ref, s_ref, p_ref):
    n, pc = k_ref.shape[1], kc_ref.shape[1]
    chunks = [(k_ref, vt_ref, off, MLA_KCHUNK, off) for off in range(0, n, MLA_KCHUNK)] + [(kc_ref, vct_ref, 0, pc, n)]

    def score_chunk(hd, ci):
        kref, _, off, w, row = chunks[ci]
        hsl = slice(hd * MLA_QK_PAD, (hd + 1) * MLA_QK_PAD)
        s = _dot_nt(kref[0, off:off + w, hsl], q_ref[0, :, hsl])
        s_ref[hd % 2, row:row + w, :] = s
        return _fold_rows(s, jnp.maximum)

    def prob_chunk(hd, ci, m):
        _, _, _, w, row = chunks[ci]
        p = jnp.exp2(s_ref[hd % 2, row:row + w, :] - m)
        p_ref[hd % 2, row:row + w, :] = p.astype(BF)
        return _fold_rows(p, jnp.add)

    def value_chunk(hd, ci):
        _, vref, off, w, row = chunks[ci]
        return _dot(vref[0, hd * MLA_V:(hd + 1) * MLA_V, off:off + w], p_ref[hd % 2, row:row + w, :])

    pending = {}

    def emit(hd, o_t):
        if hd % 2 == 0:
            pending[hd] = o_t
        else:
            pair_t = jnp.concatenate([pending.pop(hd - 1), o_t], axis=0)
            o_ref[0, :, (hd // 2) * LANES:(hd // 2 + 1) * LANES] = pair_t.T.astype(BF)

    _pipelined_softmax(MLA_HEADS, len(chunks), 0, score_chunk, prob_chunk, value_chunk, emit)


def _lat_mla(qm, km, vm_t, kc, vc_t, tq):
    b, n, _ = qm.shape
    p = kc.shape[1]
    nv = MLA_HEADS * MLA_V
    return pl.pallas_call(
        _lat_mla_kernel,
        scratch_shapes=[pltpu.VMEM((2, n + p, tq), F32), pltpu.VMEM((2, n + p, tq), BF)],
        out_shape=jax.ShapeDtypeStruct((b, n, BRANCH_W), BF),
        grid=(b, n // tq),
        in_specs=[
            pl.BlockSpec((1, tq, 1024), lambda i, j: (i, j, 0)),
            pl.BlockSpec((1, n, 1024), lambda i, j: (i, 0, 0)),
            pl.BlockSpec((1, nv, n), lambda i, j: (i, 0, 0)),
            pl.BlockSpec((1, p, 1024), lambda i, j: (i, 0, 0)),
            pl.BlockSpec((1, nv, p), lambda i, j: (i, 0, 0)),
        ],
        out_specs=pl.BlockSpec((1, tq, BRANCH_W), lambda i, j: (i, j, 0)),
        compiler_params=_params("arbitrary", "arbitrary"),
        name="lat_mla",
    )(qm, km, vm_t, kc, vc_t)


_RPB_ROWS = 2 * NA_WIN_H - 1
_RPB_COLS = 2 * NA_WIN_W - 1


def _na_bias_kernel(rpb_ref, o_ref, *, rows):
    base = (pl.program_id(0) * NA_HEADS + pl.program_id(1)) * (_RPB_ROWS * _RPB_COLS)
    shape = (GRID_W, 2 * GRID_W)
    kc = lax.broadcasted_iota(jnp.int32, shape, 0)
    lane = lax.broadcasted_iota(jnp.int32, shape, 1)
    qc = lane & (GRID_W - 1)
    upper = lane >= GRID_W
    ws = jnp.clip(qc - NA_WIN_W // 2, 0, GRID_W - NA_WIN_W)
    in_cols = (kc >= ws) & (kc < ws + NA_WIN_W)
    cidx = kc - qc + (NA_WIN_W - 1)
    col_is = [cidx == j for j in range(_RPB_COLS)]
    neg = jnp.full(shape, NEG_INF, F32)
    tiles = {}

    def pair_tile(ia, ib):
        if (ia, ib) not in tiles:
            if ia is None and ib is None:
                tiles[(ia, ib)] = neg
            else:
                acc = neg
                for j in range(_RPB_COLS):
                    sa = NEG_INF if ia is None else rpb_ref[base + ia * _RPB_COLS + j] * LOG2E
                    sb = NEG_INF if ib is None else rpb_ref[base + ib * _RPB_COLS + j] * LOG2E
                    acc = jnp.where(col_is[j], jnp.where(upper, sb, sa), acc)
                valid = in_cols
                if ia is None:
                    valid = valid & upper
                if ib is None:
                    valid = valid & jnp.logical_not(upper)
                tiles[(ia, ib)] = jnp.where(valid, acc, neg)
        return tiles[(ia, ib)]

    for kind, r0 in enumerate((0, 2 * NA_QROWS, rows - NA_QROWS)):
        ks = int(np.clip(r0 - NA_WIN_H // 2, 0, rows - NA_KROWS))
        for krl in range(NA_KROWS):
            for m in range(NA_QROWS // 2):
                idx = []
                for qr in (2 * m, 2 * m + 1):
                    r = r0 + qr
                    rs = int(np.clip(r - NA_WIN_H // 2, 0, rows - NA_WIN_H))
                    idx.append((ks + krl - r + NA_WIN_H - 1) if rs <= ks + krl < rs + NA_WIN_H else None)
                o_ref[0, kind, 0, krl * GRID_W:(krl + 1) * GRID_W, m * 2 * GRID_W:(m + 1) * 2 * GRID_W] = (
                    pair_tile(*idx))


def _na_bias_tables(rpb, rows):
    tq, nk = NA_QROWS * GRID_W, NA_KROWS * GRID_W
    return pl.pallas_call(
        functools.partial(_na_bias_kernel, rows=rows),
        out_shape=jax.ShapeDtypeStruct((DEPTH, 3, NA_HEADS, nk, tq), F32),
        grid=(DEPTH, NA_HEADS),
        in_specs=[pl.BlockSpec(memory_space=pltpu.SMEM)],
        out_specs=pl.BlockSpec((1, 3, 1, nk, tq), lambda l, h: (l, 0, h, 0, 0)),
        compiler_params=_params("arbitrary", "arbitrary"),
        name="na_bias",
    )(rpb.reshape(-1))


def _lat_na_kernel(q_ref, k_ref, vt_ref, kc_ref, vct_ref, bias_ref, o_ref, s_ref, p_ref, *, rows):
    g = pl.program_id(1)
    kb0 = jnp.clip((NA_QROWS * GRID_W // LANES) * g - NA_WIN_H // 2 * GRID_W // LANES,
                   0, (rows - NA_KROWS) * GRID_W // LANES)
    k0 = pl.multiple_of(kb0 * LANES, LANES)
    nk = NA_KROWS * GRID_W
    pc = kc_ref.shape[1]
    zero = jnp.zeros((), BF)
    local_chunks = [(off, min(NA_KCHUNK, nk - off)) for off in range(0, nk, NA_KCHUNK)]
    n_chunks = len(local_chunks) + 1

    def pair(hd):
        return slice((hd // 2) * LANES, (hd // 2 + 1) * LANES)

    def score_chunk(hd, ci):
        qh = jnp.where(_half_mask(LANES, hd % 2), q_ref[0, :, pair(hd)], zero)
        if ci < len(local_chunks):
            off, w = local_chunks[ci]
            kblk = k_ref[0, pl.ds(pl.multiple_of(k0 + off, LANES), w), pair(hd)]
            s = _dot_nt(kblk, qh) + bias_ref[0, hd, off:off + w, :]
        else:
            off, w = nk, pc
            s = _dot_nt(kc_ref[0, :, pair(hd)], qh)
        s_ref[hd % 2, off:off + w, :] = s
        return _fold_rows(s, jnp.maximum)

    def prob_chunk(hd, ci, m):
        off, w = local_chunks[ci] if ci < len(local_chunks) else (nk, pc)
        p = jnp.exp2(s_ref[hd % 2, off:off + w, :] - m)
        p_ref[hd % 2, off:off + w, :] = p.astype(BF)
        return _fold_rows(p, jnp.add)

    def value_chunk(hd, ci):
        rows_h = slice(hd * NA_HEAD_DIM, (hd + 1) * NA_HEAD_DIM)
        if ci < len(local_chunks):
            off, w = local_chunks[ci]
            v_t = jnp.concatenate([vt_ref[0, kb0 + off // LANES + i, rows_h, :] for i in range(w // LANES)], axis=1)
        else:
            off, w = nk, pc
            v_t = vct_ref[0, rows_h, :]
        return _dot(v_t, p_ref[hd % 2, off:off + w, :])

    pending = {}

    def emit(hd, o_t):
        if hd % 2 == 0:
            pending[hd] = o_t
        else:
            pair_t = jnp.concatenate([pending.pop(hd - 1), o_t], axis=0)
            o_ref[0, :, pair(hd)] = pair_t.T.astype(BF)

    _pipelined_softmax(NA_HEADS, n_chunks, 0, score_chunk, prob_chunk, value_chunk, emit)


def _lat_na(q, k, v_t, kc, vc_t, bias, layer):
    b, n, _ = q.shape
    p = kc.shape[2]
    rows = n // GRID_W
    ngroups = rows // NA_QROWS
    tq = NA_QROWS * GRID_W
    nk = NA_KROWS * GRID_W

    def bias_map(i, g):
        kind = jnp.where(g == 0, 0, jnp.where(g == ngroups - 1, 2, 1))
        return (layer, kind, 0, 0, 0)

    return pl.pallas_call(
        functools.partial(_lat_na_kernel, rows=rows),
        out_shape=jax.ShapeDtypeStruct((b, n, BRANCH_W), BF),
        scratch_shapes=[pltpu.VMEM((2, nk + p, tq), F32), pltpu.VMEM((2, nk + p, tq), BF)],
        grid=(b, ngroups),
        in_specs=[
            pl.BlockSpec((1, tq, 512), lambda i, g: (i, g, 0)),
            pl.BlockSpec((1, n, 512), lambda i, g: (i, 0, 0)),
            pl.BlockSpec((1, n // LANES, 512, LANES), lambda i, g: (i, 0, 0, 0)),
            pl.BlockSpec((1, None, p, 512), lambda i, g: (i, layer, 0, 0)),
            pl.BlockSpec((1, None, 512, p), lambda i, g: (i, layer, 0, 0)),
            pl.BlockSpec((None, 1, NA_HEADS, nk, tq), bias_map),
        ],
        out_specs=pl.BlockSpec((1, tq, BRANCH_W), lambda i, g: (i, g, 0)),
        compiler_params=_params("arbitrary", "arbitrary"),
        name="lat_na",
    )(q, k, v_t, kc, vc_t, bias)


def _merge_kernel(x_ref, mod_ref, nrm_ref, na_ref, mla_ref, pool_ref, prev_ref, next_ref,
                  wg_ref, wb_ref, wo_ref, pw_ref, ps_ref, o_ref, ext_ref, *, seq_len):
    tm = x_ref.shape[1]
    t = pl.program_id(1)
    nt = pl.num_programs(1)
    y_na = _dot(na_ref[0], wb_ref[0])
    y_mla = _dot(mla_ref[0], wb_ref[2])
    x = x_ref[0]
    shift, scale, gate = mod_ref[0, 0:1, :], mod_ref[0, 1:2, :], mod_ref[0, 2:3, :]
    u = (_rms(x, nrm_ref[0:1, :]) * (1.0 + scale) + shift).astype(BF)

    cur = pool_ref[0]
    ext_ref[0:POOL_HALO, :] = jnp.where(t > 0, prev_ref[0], 0.0)
    ext_ref[POOL_HALO:POOL_HALO + tm, :] = cur
    ext_ref[POOL_HALO + tm:POOL_HALO + tm + POOL_HALO, :] = jnp.where(t < nt - 1, next_ref[0], 0.0)
    pos = t * tm + lax.broadcasted_iota(jnp.int32, (tm, 1), 0)
    pooled = []
    for gi, w in enumerate(POOL_WINDOWS):
        sl = slice(gi * POOL_GROUP_DIM, (gi + 1) * POOL_GROUP_DIM)
        tot = None
        for j in range(-(w // 2), w // 2):
            piece = ext_ref[POOL_HALO + j:POOL_HALO + j + tm, sl]
            tot = piece if tot is None else tot + piece
        cnt = (jnp.minimum(pos + w // 2, seq_len) - jnp.maximum(pos - w // 2, 0)).astype(F32)
        diff = (tot / cnt - cur[:, sl]).astype(BF)
        pooled.append(_dot(diff, pw_ref[gi]))
    pool_o = (jnp.concatenate(pooled, axis=-1) * ps_ref[...]).astype(BF)

    ys = (y_na, _dot(pool_o, wb_ref[1]), y_mla)
    z = None
    for kb in range(3):
        gk = _sigmoid(_dot(u, wg_ref[:, kb * D_MODEL:(kb + 1) * D_MODEL]))
        z = gk * ys[kb] if z is None else z + gk * ys[kb]
    y = _dot(z.astype(BF), wo_ref[...])
    o_ref[0] = x + gate * _rms(y, nrm_ref[1:2, :])


def _merge(x, mods3, nrm2, na_o, mla_o, pool_in, wts, tm):
    b, n, _ = x.shape
    nh = tm // POOL_HALO
    last_h = n // POOL_HALO - 1
    tspec = lambda w: pl.BlockSpec((1, tm, w), lambda i, j: (i, j, 0))
    return pl.pallas_call(
        functools.partial(_merge_kernel, seq_len=n),
        out_shape=jax.ShapeDtypeStruct(x.shape, F32),
        grid=(b, n // tm),
        in_specs=[
            tspec(D_MODEL),
            pl.BlockSpec((1, 3, D_MODEL), lambda i, j: (i, 0, 0)),
            _const_spec((2, D_MODEL)),
            tspec(BRANCH_W), tspec(BRANCH_W), tspec(512),
            pl.BlockSpec((1, POOL_HALO, 512), lambda i, j: (i, jnp.maximum(j * nh - 1, 0), 0)),
            pl.BlockSpec((1, POOL_HALO, 512), lambda i, j: (i, jnp.minimum((j + 1) * nh, last_h), 0)),
            _const_spec((D_MODEL, 3 * D_MODEL)),
            _const_spec((3, BRANCH_W, D_MODEL)),
            _const_spec((D_MODEL, D_MODEL)),
            _const_spec((POOL_GROUPS, POOL_GROUP_DIM, POOL_GROUP_DIM)),
            _const_spec((1, 512)),
        ],
        out_specs=tspec(D_MODEL),
        scratch_shapes=[pltpu.VMEM((tm + 2 * POOL_HALO, 512), F32)],
        compiler_params=_params("arbitrary", "arbitrary"),
        name="merge",
    )(x, mods3, nrm2, na_o, mla_o, pool_in, pool_in, pool_in,
      wts["w_g"], wts["w_branch"], wts["w_out"], wts["pool_w"], wts["pool_scale"])


def _rope_tables(n_tok):
    half = MLA_ROPE // 2
    n_freq = half // 2
    inv = ROPE_BASE ** (-jnp.arange(n_freq, dtype=F32) / n_freq)
    t = jnp.arange(n_tok)
    ang_r = (t // GRID_W).astype(F32)[:, None] * inv
    ang_c = (t % GRID_W).astype(F32)[:, None] * inv
    ones = jnp.ones((n_tok, MLA_NOPE), F32)
    tail = jnp.zeros((n_tok, MLA_QK_PAD - MLA_NOPE - MLA_ROPE), F32)
    cos = jnp.concatenate([ones, jnp.cos(ang_r), jnp.cos(ang_r), jnp.cos(ang_c), jnp.cos(ang_c), tail + 1.0], axis=-1)
    sin = jnp.concatenate([0.0 * ones, -jnp.sin(ang_r), jnp.sin(ang_r), -jnp.sin(ang_c), jnp.sin(ang_c), tail], axis=-1)
    return cos, sin


def _rope_partner_perm():
    n_freq = MLA_ROPE // 4
    idx = np.arange(MLA_ROPE)
    return np.where((idx // n_freq) % 2 == 0, idx + n_freq, idx - n_freq)


def _pad_heads(w, parts):
    r, h, _ = w.shape
    cols = jnp.concatenate(parts, axis=-1)
    pad = MLA_QK_PAD - cols.shape[-1]
    return jnp.pad(cols, ((0, 0), (0, 0), (0, pad))).reshape(r, h * MLA_QK_PAD)


def _layer_weights(l, w_in, pool_w, pool_scale, q_norm, kv_norm, w_uq, w_ukv, w_branch, w_out):
    perm = _rope_partner_perm()
    bounds = np.cumsum([512, 512, 512, 512, MLA_Q_LORA, MLA_KV_LORA, MLA_ROPE])
    wi = w_in[l]
    kr_cols = wi[:, bounds[5]:bounds[6]]
    lane_pad = lambda c: jnp.pad(c, ((0, 0), (MLA_NOPE, MLA_QK_PAD - MLA_NOPE - MLA_ROPE)))
    w1 = jnp.concatenate([wi[:, :bounds[5]], lane_pad(kr_cols), lane_pad(kr_cols[:, perm])], axis=-1).astype(BF)
    uq = w_uq[l].reshape(MLA_Q_LORA, MLA_HEADS, MLA_NOPE + MLA_ROPE)
    ukv = w_ukv[l].reshape(MLA_KV_LORA, MLA_HEADS, MLA_NOPE + MLA_V)
    zeros_nope = jnp.zeros((MLA_Q_LORA, MLA_HEADS, MLA_NOPE), F32)
    return {
        "w1": w1,
        "w_g": wi[:, bounds[6]:].astype(BF),
        "q_norm": q_norm[l][None, :],
        "kv_norm": kv_norm[l][None, :],
        "w_uq": _pad_heads(uq, [uq]).astype(BF),
        "w_uq_sw": _pad_heads(uq, [zeros_nope, uq[..., MLA_NOPE:][..., perm]]).astype(BF),
        "w_uk": _pad_heads(ukv, [ukv[..., :MLA_NOPE]]).astype(BF),
        "w_uv": ukv[..., MLA_NOPE:].reshape(MLA_KV_LORA, MLA_HEADS * MLA_V).astype(BF),
        "w_uv_t": ukv[..., MLA_NOPE:].reshape(MLA_KV_LORA, MLA_HEADS * MLA_V).T.astype(BF),
        "w_branch": w_branch[l].astype(BF),
        "w_out": w_out[l].astype(BF),
        "pool_w": pool_w[l].astype(BF),
        "pool_scale": pool_scale[l][None, :],
    }


def kernel(x_prompt, x_sample, cache_na_k, cache_na_v, cache_mla_ckv, cache_mla_krope, c, c_ctx, w_ada, b_ada, norm_pre, norm_post, ffn_w_in, ffn_w_out, w_in, na_rpb, pool_w, pool_scale, mla_q_norm, mla_kv_norm, mla_w_uq, mla_w_ukv, w_branch, w_out):
    batch, seq, _ = x_prompt.shape
    dec_batch, dec_seq, _ = x_sample.shape
    past = cache_na_k.shape[2]
    rows = dec_seq // GRID_W

    n_rows = -(-(1 + dec_batch) // 8) * 8
    cvec = jnp.concatenate([c_ctx[None, :], c, jnp.zeros((n_rows - 1 - dec_batch, D_MODEL), F32)], axis=0)
    mods = _modulations(cvec, w_ada, b_ada)

    lw = [_layer_weights(l, w_in, pool_w, pool_scale, mla_q_norm, mla_kv_norm, mla_w_uq, mla_w_ukv,
                         w_branch, w_out) for l in range(DEPTH)]
    rope = _rope_tables(dec_seq)
    bias = _na_bias_tables(na_rpb, rows)
    ffn_in = ffn_w_in.astype(BF)
    ffn_out = ffn_w_out.astype(BF)

    kc_na = cache_na_k.reshape(dec_batch, DEPTH, past, 512).astype(BF)
    vc_na = cache_na_v.reshape(dec_batch, DEPTH, past, 512).transpose(0, 1, 3, 2).astype(BF)
    kr_c = jnp.pad(cache_mla_krope, ((0, 0), (0, 0), (0, 0), (MLA_NOPE, MLA_QK_PAD - MLA_NOPE - MLA_ROPE)))
    kc_mla, vc_mla = _cache_kv(cache_mla_ckv, kr_c, jnp.stack([w["w_uk"] for w in lw]),
                               jnp.stack([w["w_uv_t"] for w in lw]))

    xp = x_prompt.reshape(1, batch * seq, D_MODEL)
    xs = x_sample
    states = [[], [], [], []]
    tm = 512
    for l in range(DEPTH):
        m_ctx, m_lat = mods[l, 0:1], mods[l, 1:1 + dec_batch]
        nrm = lambda j: jnp.stack([norm_pre[l, j], norm_post[l, j]])
        w = lw[l]
        xp = _ffn(xp, m_ctx[:, 0:3], nrm(0), ffn_in[l, 0], ffn_out[l, 0], tm)
        xs = _ffn(xs, m_lat[:, 0:3], nrm(0), ffn_in[l, 0], ffn_out[l, 0], tm)

        q, k, v, pool_in, qm, km, vm, ckv, kr = _inproj(
            xp, m_ctx[:, 3:6], norm_pre[l, 1][None, :], w, None, tm, latent=False)
        per_seq = lambda a: a.reshape(batch, seq, a.shape[-1])
        na_o, mla_o = _ctx_attn(*(per_seq(a) for a in (q, k, v, qm, km, vm)))
        xp = _merge(xp.reshape(batch, seq, D_MODEL), jnp.broadcast_to(m_ctx[:, 3:6], (batch, 3, D_MODEL)),
                    nrm(1), na_o, mla_o, per_seq(pool_in), w, seq).reshape(1, batch * seq, D_MODEL)
        states[0].append(k.reshape(batch, seq, NA_HEADS, NA_HEAD_DIM))
        states[1].append(v.reshape(batch, seq, NA_HEADS, NA_HEAD_DIM))
        states[2].append(ckv.reshape(batch, seq, MLA_KV_LORA))
        states[3].append(kr.reshape(batch, seq, LANES)[..., MLA_NOPE:MLA_NOPE + MLA_ROPE])

        q, k, v, pool_in, qm, km, vm = _inproj(xs, m_lat[:, 3:6], norm_pre[l, 1][None, :], w, rope, tm, latent=True)
        na_o = _lat_na(q, k, v, kc_na, vc_na, bias, l)
        mla_o = _lat_mla(qm, km, vm, kc_mla[l], vc_mla[l], 256)
        xs = _merge(xs, m_lat[:, 3:6], nrm(1), na_o, mla_o, pool_in, w, tm)

        xp = _ffn(xp, m_ctx[:, 6:9], nrm(2), ffn_in[l, 1], ffn_out[l, 1], tm)
        xs = _ffn(xs, m_lat[:, 6:9], nrm(2), ffn_in[l, 1], ffn_out[l, 1], tm)

    return (xp.reshape(batch, seq, D_MODEL), xs) + tuple(jnp.stack(s, axis=1) for s in states)
```

```python
import functools

import numpy as np
import jax
import jax.numpy as jnp
from jax import lax
from jax.experimental import pallas as pl
from jax.experimental.pallas import tpu as pltpu

D_MODEL = 1024
DEPTH = 2
GRID_W = 64
N_MOD = 9
FFN_DIM = 2816
BRANCH_W = 512
NA_HEADS = 8
NA_HEAD_DIM = 64
NA_WIN_H = 8
NA_WIN_W = 16
POOL_GROUPS = 4
POOL_GROUP_DIM = 128
POOL_WINDOWS = (2, 4, 8, 16)
POOL_HALO = 8
MLA_HEADS = 8
MLA_Q_LORA = 256
MLA_KV_LORA = 256
MLA_NOPE = 64
MLA_ROPE = 32
MLA_V = 64
ROPE_BASE = 10000.0
NORM_EPS = 1e-6
LOG2E = 1.4426950408889634
NEG_INF = -1e30

LANES = 128
MLA_QK_PAD = LANES
VMEM_LIMIT_BYTES = 56 * 1024 * 1024

NA_QROWS = 4
NA_KROWS = 12
NA_KCHUNK = 512

BF = jnp.bfloat16
F32 = jnp.float32

_NT = (((1,), (1,)), ((), ()))


def _dot(a, b):
    return jnp.dot(a, b, preferred_element_type=F32)


def _dot_nt(a, b):
    return lax.dot_general(a, b, _NT, preferred_element_type=F32)


def _sigmoid(x):
    return 1.0 / (1.0 + jnp.exp(-x))


def _silu(x):
    return x * _sigmoid(x)


def _rms(x, g):
    ms = jnp.mean(x * x, axis=-1, keepdims=True)
    return (x * lax.rsqrt(ms + NORM_EPS)) * g


def _const_spec(shape):
    nd = len(shape)
    return pl.BlockSpec(shape, lambda *_: (0,) * nd, pipeline_mode=pl.Buffered(1))


def _params(*sem):
    return pltpu.CompilerParams(dimension_semantics=sem, vmem_limit_bytes=VMEM_LIMIT_BYTES)


def _mods_kernel(c_ref, w_ref, b_ref, o_ref):
    a = _silu(c_ref[...]).astype(BF)
    o_ref[0] = _dot(a, w_ref[0].astype(BF)) + b_ref[0]


def _modulations(cvec, w_ada, b_ada):
    r = cvec.shape[0]
    ncol = N_MOD * D_MODEL
    tn = 1024
    out = pl.pallas_call(
        _mods_kernel,
        out_shape=jax.ShapeDtypeStruct((DEPTH, r, ncol), F32),
        grid=(DEPTH, ncol // tn),
        in_specs=[
            pl.BlockSpec((r, D_MODEL), lambda l, j: (0, 0)),
            pl.BlockSpec((1, D_MODEL, tn), lambda l, j: (l, 0, j)),
            pl.BlockSpec((1, 1, tn), lambda l, j: (l, 0, j)),
        ],
        out_specs=pl.BlockSpec((1, r, tn), lambda l, j: (l, 0, j)),
        compiler_params=_params("arbitrary", "arbitrary"),
        name="mods",
    )(cvec, w_ada, b_ada.reshape(DEPTH, 1, ncol))
    return out.reshape(DEPTH, r, N_MOD, D_MODEL)


_FFN_CHUNKS = ((0, 512), (512, 512), (1024, 512), (1536, 512), (2048, 512), (2560, 256))


def _ffn_kernel(x_ref, mod_ref, nrm_ref, win_ref, wout_ref, o_ref):
    x = x_ref[0]
    shift, scale, gate = mod_ref[0, 0:1, :], mod_ref[0, 1:2, :], mod_ref[0, 2:3, :]
    u = (_rms(x, nrm_ref[0:1, :]) * (1.0 + scale) + shift).astype(BF)
    acc = None
    for off, ck in _FFN_CHUNKS:
        g = _dot(u, win_ref[:, off:off + ck])
        up = _dot(u, win_ref[:, FFN_DIM + off:FFN_DIM + off + ck])
        d = _dot((_silu(g) * up).astype(BF), wout_ref[off:off + ck, :])
        acc = d if acc is None else acc + d
    o_ref[0] = x + 0.5 * (gate * _rms(acc, nrm_ref[1:2, :]))


def _ffn(x, mods3, nrm2, w_in, w_out, tm):
    b, n, _ = x.shape
    return pl.pallas_call(
        _ffn_kernel,
        out_shape=jax.ShapeDtypeStruct(x.shape, F32),
        grid=(b, n // tm),
        in_specs=[
            pl.BlockSpec((1, tm, D_MODEL), lambda i, j: (i, j, 0)),
            pl.BlockSpec((1, 3, D_MODEL), lambda i, j: (i, 0, 0)),
            _const_spec((2, D_MODEL)),
            _const_spec((D_MODEL, 2 * FFN_DIM)),
            _const_spec((FFN_DIM, D_MODEL)),
        ],
        out_specs=pl.BlockSpec((1, tm, D_MODEL), lambda i, j: (i, j, 0)),
        compiler_params=_params("arbitrary", "arbitrary"),
        name="ffn",
    )(x, mods3, nrm2, w_in, w_out)


_C_Q, _C_K, _C_V, _C_POOL, _C_CQ, _C_CKV, _C_KR, _C_KRS, _C_END = (
    0, 512, 1024, 1536, 2048, 2304, 2560, 2688, 2816)


def _inproj_kernel(*refs, latent):
    if latent:
        (x_ref, mod_ref, g_ref, w1_ref, qn_ref, kvn_ref, wuq_ref, wuqs_ref, wuk_ref, wuv_ref,
         cos_ref, sin_ref,
         q_ref, k_ref, v_ref, pool_ref, qm_ref, km_ref, vm_ref) = refs
    else:
        (x_ref, mod_ref, g_ref, w1_ref, qn_ref, kvn_ref, wuq_ref, wuqs_ref, wuk_ref, wuv_ref,
         q_ref, k_ref, v_ref, pool_ref, qm_ref, km_ref, vm_ref, ckv_ref, kr_ref) = refs
    x = x_ref[0]
    shift, scale = mod_ref[0, 0:1, :], mod_ref[0, 1:2, :]
    u = (_rms(x, g_ref[...]) * (1.0 + scale) + shift).astype(BF)
    h = _dot(u, w1_ref[...])
    q_ref[0] = (h[:, _C_Q:_C_K] * (NA_HEAD_DIM ** -0.5 * LOG2E)).astype(q_ref.dtype)
    k_ref[0] = h[:, _C_K:_C_V].astype(k_ref.dtype)
    if latent:
        v_t = h[:, _C_V:_C_POOL].T.astype(BF)
        for t in range(v_ref.shape[1]):
            v_ref[0, t] = v_t[:, t * LANES:(t + 1) * LANES]
    else:
        v_ref[0] = h[:, _C_V:_C_POOL]
    pool_ref[0] = h[:, _C_POOL:_C_CQ]

    qn = _rms(h[:, _C_CQ:_C_CKV], qn_ref[...]).astype(BF)
    ckv = _rms(h[:, _C_CKV:_C_KR], kvn_ref[...])
    ckv_b = ckv.astype(BF)
    kr = h[:, _C_KR:_C_KRS]
    qa = _dot(qn, wuq_ref[...])
    if latent:
        cos, sin = cos_ref[...], sin_ref[...]
        qb = _dot(qn, wuqs_ref[...])
        kr = kr * cos + h[:, _C_KRS:_C_END] * sin
    else:
        ckv_ref[0] = ckv
        kr_ref[0] = kr
    kn = _dot(ckv_b, wuk_ref[...])
    if latent:
        vm_ref[0] = _dot_nt(wuv_ref[...], ckv_b).astype(BF)
    else:
        vm_ref[0] = _dot(ckv_b, wuv_ref[...]).astype(BF)
    qscale =(MLA_NOPE + MLA_ROPE) ** -0.5 * LOG2E
    for hd in range(MLA_HEADS):
        sl = slice(hd * MLA_QK_PAD, (hd + 1) * MLA_QK_PAD)
        qh = qa[:, sl]
        if latent:
            qh = qh * cos + qb[:, sl] * sin
        qm_ref[0, :, sl] = (qh * qscale).astype(BF)
        km_ref[0, :, sl] = (kn[:, sl] + kr).astype(BF)


def _inproj(x, mods3, g_pre, wts, rope, tm, latent):
    b, n, _ = x.shape
    tok = lambda w, dt: jax.ShapeDtypeStruct((b, n, w), dt)
    tspec = lambda w: pl.BlockSpec((1, tm, w), lambda i, j: (i, j, 0))
    kv_dt = BF if latent else F32
    nv = MLA_HEADS * MLA_V
    out_shape = [tok(512, BF), tok(512, kv_dt),
                 jax.ShapeDtypeStruct((b, n // LANES, 512, LANES), BF) if latent else tok(512, F32),
                 tok(512, F32), tok(1024, BF), tok(1024, BF),
                 jax.ShapeDtypeStruct((b, nv, n), BF) if latent else tok(nv, BF)]
    out_specs = [tspec(512), tspec(512),
                 pl.BlockSpec((1, tm // LANES, 512, LANES), lambda i, j: (i, j, 0, 0)) if latent else tspec(512),
                 tspec(512), tspec(1024), tspec(1024),
                 pl.BlockSpec((1, nv, tm), lambda i, j: (i, 0, j)) if latent else tspec(nv)]
    in_specs = [
        tspec(D_MODEL),
        pl.BlockSpec((1, 3, D_MODEL), lambda i, j: (i, 0, 0)),
        _const_spec((1, D_MODEL)),
        _const_spec((D_MODEL, _C_END)),
        _const_spec((1, MLA_Q_LORA)),
        _const_spec((1, MLA_KV_LORA)),
        _const_spec((MLA_Q_LORA, MLA_HEADS * MLA_QK_PAD)),
        _const_spec((MLA_Q_LORA, MLA_HEADS * MLA_QK_PAD)),
        _const_spec((MLA_KV_LORA, MLA_HEADS * MLA_QK_PAD)),
        _const_spec((nv, MLA_KV_LORA) if latent else (MLA_KV_LORA, nv)),
    ]
    args = [x, mods3, g_pre, wts["w1"], wts["q_norm"], wts["kv_norm"], wts["w_uq"], wts["w_uq_sw"],
            wts["w_uk"], wts["w_uv_t"] if latent else wts["w_uv"]]
    if latent:
        in_specs += [pl.BlockSpec((tm, LANES), lambda i, j: (j, 0))] * 2
        args += [rope[0], rope[1]]
    else:
        out_shape += [tok(MLA_KV_LORA, F32), tok(LANES, F32)]
        out_specs += [tspec(MLA_KV_LORA), tspec(LANES)]
    return pl.pallas_call(
        functools.partial(_inproj_kernel, latent=latent),
        out_shape=out_shape,
        grid=(b, n // tm),
        in_specs=in_specs,
        out_specs=out_specs,
        compiler_params=_params("arbitrary", "arbitrary"),
        name="inproj_lat" if latent else "inproj_ctx",
    )(*args)


def _cache_kv_kernel(ckv_ref, kr_ref, wuk_ref, wuv_ref, km_ref, vm_ref):
    ckv = ckv_ref[0, 0].astype(BF)
    kr = kr_ref[0, 0]
    kn = _dot(ckv, wuk_ref[0])
    vm_ref[0, 0] = _dot_nt(wuv_ref[0], ckv).astype(BF)
    for hd in range(MLA_HEADS):
        sl = slice(hd * MLA_QK_PAD, (hd + 1) * MLA_QK_PAD)
        km_ref[0, 0, :, sl] = (kn[:, sl] + kr).astype(BF)


def _cache_kv(ckv_c, kr_pad_c, w_uk, w_uv_t):
    b, _, p, _ = ckv_c.shape
    nv = MLA_HEADS * MLA_V
    in_spec = lambda r, w: pl.BlockSpec((1, 1, r, w), lambda l, i: (i, l, 0, 0))
    spec = lambda r, w: pl.BlockSpec((1, 1, r, w), lambda l, i: (l, i, 0, 0))
    wspec = lambda r, w: pl.BlockSpec((1, r, w), lambda l, i: (l, 0, 0))
    return pl.pallas_call(
        _cache_kv_kernel,
        out_shape=[jax.ShapeDtypeStruct((DEPTH, b, p, MLA_HEADS * MLA_QK_PAD), BF),
                   jax.ShapeDtypeStruct((DEPTH, b, nv, p), BF)],
        grid=(DEPTH, b),
        in_specs=[in_spec(p, MLA_KV_LORA), in_spec(p, LANES),
                  wspec(MLA_KV_LORA, MLA_HEADS * MLA_QK_PAD), wspec(nv, MLA_KV_LORA)],
        out_specs=[spec(p, MLA_HEADS * MLA_QK_PAD), spec(nv, p)],
        compiler_params=_params("arbitrary", "arbitrary"),
        name="cache_kv",
    )(ckv_c, kr_pad_c, w_uk, w_uv_t)


def _half_mask(width, j):
    lane = lax.broadcasted_iota(jnp.int32, (1, width), 1)
    return (lane // NA_HEAD_DIM) == j


def _softmax_pv(scores, values):
    m = None
    for s in scores:
        mi = jnp.max(s, axis=-1, keepdims=True)
        m = mi if m is None else jnp.maximum(m, mi)
    l = None
    o = None
    for s, v in zip(scores, values):
        p = jnp.exp2(s - m)
        li = jnp.sum(p, axis=-1, keepdims=True)
        oi = _dot(p.astype(BF), v)
        l = li if l is None else l + li
        o = oi if o is None else o + oi
    return o * (1.0 / l)


def _fold_rows(x, op):
    out = x[0:8, :]
    for t in range(1, x.shape[0] // 8):
        out = op(out, x[t * 8:(t + 1) * 8, :])
    return out


def _pipelined_softmax(n_heads, n_chunks, score_chunk, prob_chunk, value_chunk, emit):
    key_axis = 0

    def combine(acc, x, op):
        return x if acc is None else op(acc, x)

    mrun = None
    for ci in range(n_chunks):
        mrun = combine(mrun, score_chunk(0, ci), jnp.maximum)
    m_next = jnp.max(mrun, axis=key_axis, keepdims=True)
    inv_l = None
    for it in range(n_heads + 1):
        m, mrun, lrun, pv = m_next, None, None, None
        for ci in range(n_chunks):
            if it + 1 < n_heads:
                mrun = combine(mrun, score_chunk(it + 1, ci), jnp.maximum)
            if it < n_heads:
                lrun = combine(lrun, prob_chunk(it, ci, m), jnp.add)
            if it >= 1:
                pv = combine(pv, value_chunk(it - 1, ci), jnp.add)
        if it + 1 < n_heads:
            m_next = jnp.max(mrun, axis=key_axis, keepdims=True)
        if it >= 1:
            emit(it - 1, pv * inv_l)
        if it < n_heads:
            inv_l = 1.0 / jnp.sum(lrun, axis=key_axis, keepdims=True)


def _ctx_attn_kernel(q_ref, k_ref, v_ref, qm_ref, km_ref, vm_ref, na_ref, mla_ref):
    zero = jnp.zeros((), BF)
    for hp in range(NA_HEADS // 2):
        sl = slice(hp * LANES, (hp + 1) * LANES)
        qblk = q_ref[0, :, sl]
        kblk = k_ref[0, :, sl].astype(BF)
        vblk = v_ref[0, :, sl].astype(BF)
        vmblk = vm_ref[0, :, sl]
        acc_na = None
        acc_mla = None
        for j in range(2):
            hm = _half_mask(LANES, j)
            s = _dot_nt(jnp.where(hm, qblk, zero), kblk)
            o = _softmax_pv([s], [jnp.where(hm, vblk, zero)])
            acc_na = o if acc_na is None else acc_na + o
            hd = 2 * hp + j
            hsl = slice(hd * MLA_QK_PAD, (hd + 1) * MLA_QK_PAD)
            s = _dot_nt(qm_ref[0, :, hsl], km_ref[0, :, hsl])
            o = _softmax_pv([s], [jnp.where(hm, vmblk, zero)])
            acc_mla = o if acc_mla is None else acc_mla + o
        na_ref[0, :, sl] = acc_na.astype(BF)
        mla_ref[0, :, sl] = acc_mla.astype(BF)


def _ctx_attn(q, k, v, qm, km, vm):
    b, s, _ = q.shape
    spec = lambda w: pl.BlockSpec((1, s, w), lambda i: (i, 0, 0))
    return pl.pallas_call(
        _ctx_attn_kernel,
        out_shape=[jax.ShapeDtypeStruct((b, s, BRANCH_W), BF)] * 2,
        grid=(b,),
        in_specs=[spec(512), spec(512), spec(512), spec(1024), spec(1024), spec(512)],
        out_specs=[spec(BRANCH_W), spec(BRANCH_W)],
        compiler_params=_params("arbitrary"),
        name="ctx_attn",
    )(q, k, v, qm, km, vm)


MLA_KCHUNK = 2048


def _lat_mla_kernel(q_ref, k_ref, vt_ref, kc_ref, vct_ref, o_ref, s_ref, p_ref):
    n, pc = k_ref.shape[1], kc_ref.shape[1]
    chunks = ([(k_ref, vt_ref, off, min(MLA_KCHUNK, n - off), off) for off in range(0, n, MLA_KCHUNK)]
              + [(kc_ref, vct_ref, 0, pc, n)])

    def score_chunk(hd, ci):
        kref, _, off, w, row = chunks[ci]
        hsl = slice(hd * MLA_QK_PAD, (hd + 1) * MLA_QK_PAD)
        s = _dot_nt(kref[0, off:off + w, hsl], q_ref[0, :, hsl])
        s_ref[hd % 2, row:row + w, :] = s
        return _fold_rows(s, jnp.maximum)

    def prob_chunk(hd, ci, m):
        _, _, _, w, row = chunks[ci]
        p = jnp.exp2(s_ref[hd % 2, row:row + w, :] - m)
        p_ref[hd % 2, row:row + w, :] = p.astype(BF)
        return _fold_rows(p, jnp.add)

    def value_chunk(hd, ci):
        _, vref, off, w, row = chunks[ci]
        return _dot(vref[0, hd * MLA_V:(hd + 1) * MLA_V, off:off + w], p_ref[hd % 2, row:row + w, :])

    pending = {}

    def emit(hd, o_t):
        if hd % 2 == 0:
            pending[hd] = o_t
        else:
            pair_t = jnp.concatenate([pending.pop(hd - 1), o_t], axis=0)
            o_ref[0, :, (hd // 2) * LANES:(hd // 2 + 1) * LANES] = pair_t.T.astype(BF)

    _pipelined_softmax(MLA_HEADS, len(chunks), score_chunk, prob_chunk, value_chunk, emit)


def _lat_mla(qm, km, vm_t, kc, vc_t, tq):
    b, n, _ = qm.shape
    p = kc.shape[1]
    nv = MLA_HEADS * MLA_V
    return pl.pallas_call(
        _lat_mla_kernel,
        scratch_shapes=[pltpu.VMEM((2, n + p, tq), F32), pltpu.VMEM((2, n + p, tq), BF)],
        out_shape=jax.ShapeDtypeStruct((b, n, BRANCH_W), BF),
        grid=(b, n // tq),
        in_specs=[
            pl.BlockSpec((1, tq, 1024), lambda i, j: (i, j, 0)),
            pl.BlockSpec((1, n, 1024), lambda i, j: (i, 0, 0)),
            pl.BlockSpec((1, nv, n), lambda i, j: (i, 0, 0)),
            pl.BlockSpec((1, p, 1024), lambda i, j: (i, 0, 0)),
            pl.BlockSpec((1, nv, p), lambda i, j: (i, 0, 0)),
        ],
        out_specs=pl.BlockSpec((1, tq, BRANCH_W), lambda i, j: (i, j, 0)),
        compiler_params=_params("arbitrary", "arbitrary"),
        name="lat_mla",
    )(qm, km, vm_t, kc, vc_t)


_RPB_ROWS = 2 * NA_WIN_H - 1
_RPB_COLS = 2 * NA_WIN_W - 1


def _na_bias_kernel(rpb_ref, o_ref, *, rows):
    base = (pl.program_id(0) * NA_HEADS + pl.program_id(1)) * (_RPB_ROWS * _RPB_COLS)
    shape = (GRID_W, 2 * GRID_W)
    kc = lax.broadcasted_iota(jnp.int32, shape, 0)
    lane = lax.broadcasted_iota(jnp.int32, shape, 1)
    qc = lane & (GRID_W - 1)
    upper = lane >= GRID_W
    ws = jnp.clip(qc - NA_WIN_W // 2, 0, GRID_W - NA_WIN_W)
    in_cols = (kc >= ws) & (kc < ws + NA_WIN_W)
    cidx = kc - qc + (NA_WIN_W - 1)
    col_is = [cidx == j for j in range(_RPB_COLS)]
    neg = jnp.full(shape, NEG_INF, F32)
    tiles = {}

    def pair_tile(ia, ib):
        if (ia, ib) not in tiles:
            if ia is None and ib is None:
                tiles[(ia, ib)] = neg
            else:
                acc = neg
                for j in range(_RPB_COLS):
                    sa = NEG_INF if ia is None else rpb_ref[base + ia * _RPB_COLS + j] * LOG2E
                    sb = NEG_INF if ib is None else rpb_ref[base + ib * _RPB_COLS + j] * LOG2E
                    acc = jnp.where(col_is[j], jnp.where(upper, sb, sa), acc)
                valid = in_cols
                if ia is None:
                    valid = valid & upper
                if ib is None:
                    valid = valid & jnp.logical_not(upper)
                tiles[(ia, ib)] = jnp.where(valid, acc, neg)
        return tiles[(ia, ib)]

    for kind, r0 in enumerate((0, 2 * NA_QROWS, rows - NA_QROWS)):
        ks = int(np.clip(r0 - NA_WIN_H // 2, 0, rows - NA_KROWS))
        for krl in range(NA_KROWS):
            for m in range(NA_QROWS // 2):
                idx = []
                for qr in (2 * m, 2 * m + 1):
                    r = r0 + qr
                    rs = int(np.clip(r - NA_WIN_H // 2, 0, rows - NA_WIN_H))
                    idx.append((ks + krl - r + NA_WIN_H - 1) if rs <= ks + krl < rs + NA_WIN_H else None)
                o_ref[0, kind, 0, krl * GRID_W:(krl + 1) * GRID_W, m * 2 * GRID_W:(m + 1) * 2 * GRID_W] = (
                    pair_tile(*idx))


def _na_bias_tables(rpb, rows):
    tq, nk = NA_QROWS * GRID_W, NA_KROWS * GRID_W
    return pl.pallas_call(
        functools.partial(_na_bias_kernel, rows=rows),
        out_shape=jax.ShapeDtypeStruct((DEPTH, 3, NA_HEADS, nk, tq), F32),
        grid=(DEPTH, NA_HEADS),
        in_specs=[pl.BlockSpec(memory_space=pltpu.SMEM)],
        out_specs=pl.BlockSpec((1, 3, 1, nk, tq), lambda l, h: (l, 0, h, 0, 0)),
        compiler_params=_params("arbitrary", "arbitrary"),
        name="na_bias",
    )(rpb.reshape(-1))


def _lat_na_kernel(q_ref, k_ref, vt_ref, kc_ref, vct_ref, bias_ref, o_ref, s_ref, p_ref, *, rows):
    g = pl.program_id(1)
    kb0 = jnp.clip((NA_QROWS * GRID_W // LANES) * g - NA_WIN_H // 2 * GRID_W // LANES,
                   0, (rows - NA_KROWS) * GRID_W // LANES)
    k0 = pl.multiple_of(kb0 * LANES, LANES)
    nk = NA_KROWS * GRID_W
    pc = kc_ref.shape[1]
    zero = jnp.zeros((), BF)
    local_chunks = [(off, min(NA_KCHUNK, nk - off)) for off in range(0, nk, NA_KCHUNK)]
    n_chunks = len(local_chunks) + 1

    def pair(hd):
        return slice((hd // 2) * LANES, (hd // 2 + 1) * LANES)

    def score_chunk(hd, ci):
        qh = jnp.where(_half_mask(LANES, hd % 2), q_ref[0, :, pair(hd)], zero)
        if ci < len(local_chunks):
            off, w = local_chunks[ci]
            kblk = k_ref[0, pl.ds(pl.multiple_of(k0 + off, LANES), w), pair(hd)]
            s = _dot_nt(kblk, qh) + bias_ref[0, hd, off:off + w, :]
        else:
            off, w = nk, pc
            s = _dot_nt(kc_ref[0, :, pair(hd)], qh)
        s_ref[hd % 2, off:off + w, :] = s
        return _fold_rows(s, jnp.maximum)

    def prob_chunk(hd, ci, m):
        off, w = local_chunks[ci] if ci < len(local_chunks) else (nk, pc)
        p = jnp.exp2(s_ref[hd % 2, off:off + w, :] - m)
        p_ref[hd % 2, off:off + w, :] = p.astype(BF)
        return _fold_rows(p, jnp.add)

    def value_chunk(hd, ci):
        rows_h = slice(hd * NA_HEAD_DIM, (hd + 1) * NA_HEAD_DIM)
        if ci < len(local_chunks):
            off, w = local_chunks[ci]
            v_t = jnp.concatenate([vt_ref[0, kb0 + off // LANES + i, rows_h, :] for i in range(w // LANES)], axis=1)
        else:
            off, w = nk, pc
            v_t = vct_ref[0, rows_h, :]
        return _dot(v_t, p_ref[hd % 2, off:off + w, :])

    pending = {}

    def emit(hd, o_t):
        if hd % 2 == 0:
            pending[hd] = o_t
        else:
            pair_t = jnp.concatenate([pending.pop(hd - 1), o_t], axis=0)
            o_ref[0, :, pair(hd)] = pair_t.T.astype(BF)

    _pipelined_softmax(NA_HEADS, n_chunks, score_chunk, prob_chunk, value_chunk, emit)


def _lat_na(q, k, v_t, kc, vc_t, bias, layer):
    b, n, _ = q.shape
    p = kc.shape[2]
    rows = n // GRID_W
    ngroups = rows // NA_QROWS
    tq = NA_QROWS * GRID_W
    nk = NA_KROWS * GRID_W

    def bias_map(i, g):
        kind = jnp.where(g == 0, 0, jnp.where(g == ngroups - 1, 2, 1))
        return (layer, kind, 0, 0, 0)

    return pl.pallas_call(
        functools.partial(_lat_na_kernel, rows=rows),
        out_shape=jax.ShapeDtypeStruct((b, n, BRANCH_W), BF),
        scratch_shapes=[pltpu.VMEM((2, nk + p, tq), F32), pltpu.VMEM((2, nk + p, tq), BF)],
        grid=(b, ngroups),
        in_specs=[
            pl.BlockSpec((1, tq, 512), lambda i, g: (i, g, 0)),
            pl.BlockSpec((1, n, 512), lambda i, g: (i, 0, 0)),
            pl.BlockSpec((1, n // LANES, 512, LANES), lambda i, g: (i, 0, 0, 0)),
            pl.BlockSpec((1, None, p, 512), lambda i, g: (i, layer, 0, 0)),
            pl.BlockSpec((1, None, 512, p), lambda i, g: (i, layer, 0, 0)),
            pl.BlockSpec((None, 1, NA_HEADS, nk, tq), bias_map),
        ],
        out_specs=pl.BlockSpec((1, tq, BRANCH_W), lambda i, g: (i, g, 0)),
        compiler_params=_params("arbitrary", "arbitrary"),
        name="lat_na",
    )(q, k, v_t, kc, vc_t, bias)


def _merge_kernel(x_ref, mod_ref, nrm_ref, na_ref, mla_ref, pool_ref, prev_ref, next_ref,
                  wg_ref, wb_ref, wo_ref, pw_ref, ps_ref, o_ref, ext_ref, *, seq_len):
    tm = x_ref.shape[1]
    t = pl.program_id(1)
    nt = pl.num_programs(1)
    y_na = _dot(na_ref[0], wb_ref[0])
    y_mla = _dot(mla_ref[0], wb_ref[2])
    x = x_ref[0]
    shift, scale, gate = mod_ref[0, 0:1, :], mod_ref[0, 1:2, :], mod_ref[0, 2:3, :]
    u = (_rms(x, nrm_ref[0:1, :]) * (1.0 + scale) + shift).astype(BF)

    cur = pool_ref[0]
    ext_ref[0:POOL_HALO, :] = jnp.where(t > 0, prev_ref[0], 0.0)
    ext_ref[POOL_HALO:POOL_HALO + tm, :] = cur
    ext_ref[POOL_HALO + tm:POOL_HALO + tm + POOL_HALO, :] = jnp.where(t < nt - 1, next_ref[0], 0.0)
    pos = t * tm + lax.broadcasted_iota(jnp.int32, (tm, 1), 0)
    pooled = []
    for gi, w in enumerate(POOL_WINDOWS):
        sl = slice(gi * POOL_GROUP_DIM, (gi + 1) * POOL_GROUP_DIM)
        tot = None
        for j in range(-(w // 2), w // 2):
            piece = ext_ref[POOL_HALO + j:POOL_HALO + j + tm, sl]
            tot = piece if tot is None else tot + piece
        cnt = (jnp.minimum(pos + w // 2, seq_len) - jnp.maximum(pos - w // 2, 0)).astype(F32)
        diff = (tot / cnt - cur[:, sl]).astype(BF)
        pooled.append(_dot(diff, pw_ref[gi]))
    pool_o = (jnp.concatenate(pooled, axis=-1) * ps_ref[...]).astype(BF)

    ys = (y_na, _dot(pool_o, wb_ref[1]), y_mla)
    z = None
    for kb in range(3):
        gk = _sigmoid(_dot(u, wg_ref[:, kb * D_MODEL:(kb + 1) * D_MODEL]))
        z = gk * ys[kb] if z is None else z + gk * ys[kb]
    y = _dot(z.astype(BF), wo_ref[...])
    o_ref[0] = x + gate * _rms(y, nrm_ref[1:2, :])


def _merge(x, mods3, nrm2, na_o, mla_o, pool_in, wts, tm):
    b, n, _ = x.shape
    nh = tm // POOL_HALO
    last_h = n // POOL_HALO - 1
    tspec = lambda w: pl.BlockSpec((1, tm, w), lambda i, j: (i, j, 0))
    return pl.pallas_call(
        functools.partial(_merge_kernel, seq_len=n),
        out_shape=jax.ShapeDtypeStruct(x.shape, F32),
        grid=(b, n // tm),
        in_specs=[
            tspec(D_MODEL),
            pl.BlockSpec((1, 3, D_MODEL), lambda i, j: (i, 0, 0)),
            _const_spec((2, D_MODEL)),
            tspec(BRANCH_W), tspec(BRANCH_W), tspec(512),
            pl.BlockSpec((1, POOL_HALO, 512), lambda i, j: (i, jnp.maximum(j * nh - 1, 0), 0)),
            pl.BlockSpec((1, POOL_HALO, 512), lambda i, j: (i, jnp.minimum((j + 1) * nh, last_h), 0)),
            _const_spec((D_MODEL, 3 * D_MODEL)),
            _const_spec((3, BRANCH_W, D_MODEL)),
            _const_spec((D_MODEL, D_MODEL)),
            _const_spec((POOL_GROUPS, POOL_GROUP_DIM, POOL_GROUP_DIM)),
            _const_spec((1, 512)),
        ],
        out_specs=tspec(D_MODEL),
        scratch_shapes=[pltpu.VMEM((tm + 2 * POOL_HALO, 512), F32)],
        compiler_params=_params("arbitrary", "arbitrary"),
        name="merge",
    )(x, mods3, nrm2, na_o, mla_o, pool_in, pool_in, pool_in,
      wts["w_g"], wts["w_branch"], wts["w_out"], wts["pool_w"], wts["pool_scale"])


def _rope_tables(n_tok):
    half = MLA_ROPE // 2
    n_freq = half // 2
    inv = ROPE_BASE ** (-jnp.arange(n_freq, dtype=F32) / n_freq)
    t = jnp.arange(n_tok)
    ang_r = (t // GRID_W).astype(F32)[:, None] * inv
    ang_c = (t % GRID_W).astype(F32)[:, None] * inv
    ones = jnp.ones((n_tok, MLA_NOPE), F32)
    tail = jnp.zeros((n_tok, MLA_QK_PAD - MLA_NOPE - MLA_ROPE), F32)
    cos = jnp.concatenate([ones, jnp.cos(ang_r), jnp.cos(ang_r), jnp.cos(ang_c), jnp.cos(ang_c), tail + 1.0], axis=-1)
    sin = jnp.concatenate([0.0 * ones, -jnp.sin(ang_r), jnp.sin(ang_r), -jnp.sin(ang_c), jnp.sin(ang_c), tail], axis=-1)
    return cos, sin


def _rope_partner_perm():
    n_freq = MLA_ROPE // 4
    idx = np.arange(MLA_ROPE)
    return np.where((idx // n_freq) % 2 == 0, idx + n_freq, idx - n_freq)


def _pad_heads(w, parts):
    r, h, _ = w.shape
    cols = jnp.concatenate(parts, axis=-1)
    pad = MLA_QK_PAD - cols.shape[-1]
    return jnp.pad(cols, ((0, 0), (0, 0), (0, pad))).reshape(r, h * MLA_QK_PAD)


def _layer_weights(l, w_in, pool_w, pool_scale, q_norm, kv_norm, w_uq, w_ukv, w_branch, w_out):
    perm = _rope_partner_perm()
    bounds = np.cumsum([512, 512, 512, 512, MLA_Q_LORA, MLA_KV_LORA, MLA_ROPE])
    wi = w_in[l]
    kr_cols = wi[:, bounds[5]:bounds[6]]
    lane_pad = lambda c: jnp.pad(c, ((0, 0), (MLA_NOPE, MLA_QK_PAD - MLA_NOPE - MLA_ROPE)))
    w1 = jnp.concatenate([wi[:, :bounds[5]], lane_pad(kr_cols), lane_pad(kr_cols[:, perm])], axis=-1).astype(BF)
    uq = w_uq[l].reshape(MLA_Q_LORA, MLA_HEADS, MLA_NOPE + MLA_ROPE)
    ukv = w_ukv[l].reshape(MLA_KV_LORA, MLA_HEADS, MLA_NOPE + MLA_V)
    zeros_nope = jnp.zeros((MLA_Q_LORA, MLA_HEADS, MLA_NOPE), F32)
    return {
        "w1": w1,
        "w_g": wi[:, bounds[6]:].astype(BF),
        "q_norm": q_norm[l][None, :],
        "kv_norm": kv_norm[l][None, :],
        "w_uq": _pad_heads(uq, [uq]).astype(BF),
        "w_uq_sw": _pad_heads(uq, [zeros_nope, uq[..., MLA_NOPE:][..., perm]]).astype(BF),
        "w_uk": _pad_heads(ukv, [ukv[..., :MLA_NOPE]]).astype(BF),
        "w_uv": ukv[..., MLA_NOPE:].reshape(MLA_KV_LORA, MLA_HEADS * MLA_V).astype(BF),
        "w_uv_t": ukv[..., MLA_NOPE:].reshape(MLA_KV_LORA, MLA_HEADS * MLA_V).T.astype(BF),
        "w_branch": w_branch[l].astype(BF),
        "w_out": w_out[l].astype(BF),
        "pool_w": pool_w[l].astype(BF),
        "pool_scale": pool_scale[l][None, :],
    }


def kernel(x_prompt, x_sample, cache_na_k, cache_na_v, cache_mla_ckv, cache_mla_krope, c, c_ctx, w_ada, b_ada, norm_pre, norm_post, ffn_w_in, ffn_w_out, w_in, na_rpb, pool_w, pool_scale, mla_q_norm, mla_kv_norm, mla_w_uq, mla_w_ukv, w_branch, w_out):
    batch, seq, _ = x_prompt.shape
    dec_batch, dec_seq, _ = x_sample.shape
    past = cache_na_k.shape[2]
    rows = dec_seq // GRID_W

    n_rows = -(-(1 + dec_batch) // 8) * 8
    cvec = jnp.concatenate([c_ctx[None, :], c, jnp.zeros((n_rows - 1 - dec_batch, D_MODEL), F32)], axis=0)
    mods = _modulations(cvec, w_ada, b_ada)

    lw = [_layer_weights(l, w_in, pool_w, pool_scale, mla_q_norm, mla_kv_norm, mla_w_uq, mla_w_ukv,
                         w_branch, w_out) for l in range(DEPTH)]
    rope = _rope_tables(dec_seq)
    bias = _na_bias_tables(na_rpb, rows)
    ffn_in = ffn_w_in.astype(BF)
    ffn_out = ffn_w_out.astype(BF)

    kc_na = cache_na_k.reshape(dec_batch, DEPTH, past, 512).astype(BF)
    vc_na = cache_na_v.reshape(dec_batch, DEPTH, past, 512).transpose(0, 1, 3, 2).astype(BF)
    kr_c = jnp.pad(cache_mla_krope, ((0, 0), (0, 0), (0, 0), (MLA_NOPE, MLA_QK_PAD - MLA_NOPE - MLA_ROPE)))
    kc_mla, vc_mla = _cache_kv(cache_mla_ckv, kr_c, jnp.stack([w["w_uk"] for w in lw]),
                               jnp.stack([w["w_uv_t"] for w in lw]))

    xp = x_prompt.reshape(1, batch * seq, D_MODEL)
    xs = x_sample
    states = [[], [], [], []]
    tm = 512
    for l in range(DEPTH):
        m_ctx, m_lat = mods[l, 0:1], mods[l, 1:1 + dec_batch]
        nrm = lambda j: jnp.stack([norm_pre[l, j], norm_post[l, j]])
        w = lw[l]
        xp = _ffn(xp, m_ctx[:, 0:3], nrm(0), ffn_in[l, 0], ffn_out[l, 0], tm)
        xs = _ffn(xs, m_lat[:, 0:3], nrm(0), ffn_in[l, 0], ffn_out[l, 0], tm)

        q, k, v, pool_in, qm, km, vm, ckv, kr = _inproj(
            xp, m_ctx[:, 3:6], norm_pre[l, 1][None, :], w, None, tm, latent=False)
        per_seq = lambda a: a.reshape(batch, seq, a.shape[-1])
        na_o, mla_o = _ctx_attn(*(per_seq(a) for a in (q, k, v, qm, km, vm)))
        xp = _merge(xp.reshape(batch, seq, D_MODEL), jnp.broadcast_to(m_ctx[:, 3:6], (batch, 3, D_MODEL)),
                    nrm(1), na_o, mla_o, per_seq(pool_in), w, seq).reshape(1, batch * seq, D_MODEL)
        states[0].append(k.reshape(batch, seq, NA_HEADS, NA_HEAD_DIM))
        states[1].append(v.reshape(batch, seq, NA_HEADS, NA_HEAD_DIM))
        states[2].append(ckv.reshape(batch, seq, MLA_KV_LORA))
        states[3].append(kr.reshape(batch, seq, LANES)[..., MLA_NOPE:MLA_NOPE + MLA_ROPE])

        q, k, v, pool_in, qm, km, vm = _inproj(xs, m_lat[:, 3:6], norm_pre[l, 1][None, :], w, rope, tm, latent=True)
        na_o = _lat_na(q, k, v, kc_na, vc_na, bias, l)
        mla_o = _lat_mla(qm, km, vm, kc_mla[l], vc_mla[l], 256)
        xs = _merge(xs, m_lat[:, 3:6], nrm(1), na_o, mla_o, pool_in, w, tm)

        xp = _ffn(xp, m_ctx[:, 6:9], nrm(2), ffn_in[l, 1], ffn_out[l, 1], tm)
        xs = _ffn(xs, m_lat[:, 6:9], nrm(2), ffn_in[l, 1], ffn_out[l, 1], tm)

    return (xp.reshape(batch, seq, D_MODEL), xs) + tuple(jnp.stack(s, axis=1) for s in states)
```
